```python
import jax, jax.numpy as jnp
from jax import lax
import numpy as np

D_MODEL = 2048
BATCH = 8
SEQ = 4096
DEPTH = 1

D_MIX = D_MODEL
HEAD_DIM = 128
RET_HEADS = (D_MIX // 2) // HEAD_DIM
SB_HEADS = (D_MIX // 2) // HEAD_DIM
RET_WIDTH = RET_HEADS * HEAD_DIM
SB_WIDTH = SB_HEADS * HEAD_DIM
IN_COLS = 4 * RET_WIDTH + 4 * SB_WIDTH
CHUNK = 128
Q_BLOCK = 128
ROPE_THETA = 10000.0
EPS = 1e-6

kernel_name = "hybrid_retention_stickbreaking_block"


def rmsnorm(x, g):
    xf = x.astype(jnp.float32)
    y = xf * lax.rsqrt(jnp.mean(xf * xf, axis=-1, keepdims=True) + EPS)
    return (y * g.astype(jnp.float32)).astype(x.dtype)


def rotary(x):
    S, d = x.shape[1], x.shape[-1]
    half = d // 2
    inv = ROPE_THETA ** (-jnp.arange(half, dtype=jnp.float32) / half)
    ang = jnp.arange(S, dtype=jnp.float32)[:, None] * inv[None, :]
    cos = jnp.cos(ang)[None, :, None, :]
    sin = jnp.sin(ang)[None, :, None, :]
    xf = x.astype(jnp.float32)
    x1, x2 = xf[..., :half], xf[..., half:]
    out = jnp.concatenate([x1 * cos - x2 * sin, x1 * sin + x2 * cos], axis=-1)
    return out.astype(x.dtype)


def retention(q, k, v):
    B, S, H, d = q.shape
    nc = S // CHUNK
    dt = q.dtype
    lg = jnp.log1p(-jnp.exp2(-5.0 - jnp.arange(H, dtype=jnp.float32)))

    def to_chunks(t):
        return t.reshape(B, nc, CHUNK, H, t.shape[-1]).transpose(0, 3, 1, 2, 4)

    qc, kc, vc = to_chunks(q), to_chunks(k) * (d ** -0.5), to_chunks(v)
    n = jnp.arange(CHUNK, dtype=jnp.float32)
    rel = n[:, None] - n[None, :]
    decay = jnp.where(rel >= 0, jnp.exp(lg[:, None, None] * jnp.maximum(rel, 0.0)), 0.0)
    xi = jnp.exp(lg[:, None] * (n + 1.0))
    zeta = jnp.exp(lg[:, None] * (CHUNK - 1.0 - n))
    gamma_c = jnp.exp(lg * CHUNK).astype(dt)

    scores = jnp.einsum('bhcnd,bhcmd->bhcnm', qc, kc) * decay[None, :, None].astype(dt)
    intra = jnp.einsum('bhcnm,bhcme->bhcne', scores, vc)

    kv = jnp.einsum('bhcmd,bhcme->cbhde', kc * zeta[None, :, None, :, None].astype(dt), vc)

    def step(R, kv_c):
        return gamma_c[None, :, None, None] * R + kv_c, R

    _, R_prev = lax.scan(step, jnp.zeros_like(kv[0]), kv)
    cross = jnp.einsum('bhcnd,cbhde->bhcne', qc * xi[None, :, None, :, None].astype(dt), R_prev)
    out = intra + cross
    return out.transpose(0, 2, 3, 1, 4).reshape(B, S, H, -1)


def stick_breaking(q, k, v):
    S, d = q.shape[1], q.shape[-1]
    scale = d ** -0.5
    qh, kh, vh = (t.transpose(0, 2, 1, 3) for t in (q, k, v))
    outs = []
    for start in range(0, S, Q_BLOCK):
        end = start + Q_BLOCK
        qb, kb, vb = qh[:, :, start:end], kh[:, :, :end], vh[:, :, :end]
        z = jnp.einsum('bhtd,bhsd->bhts', qb, kb).astype(jnp.float32) * scale
        t_idx = start + jnp.arange(Q_BLOCK)[:, None]
        s_idx = jnp.arange(end)[None, :]
        causal = s_idx < t_idx
        log_keep = jnp.where(causal, jax.nn.log_sigmoid(-z), 0.0)
        log_a = jax.nn.log_sigmoid(z) + lax.cumsum(log_keep, axis=3, reverse=True) - log_keep
        a = jnp.where(causal, jnp.exp(log_a), 0.0).astype(vb.dtype)
        outs.append(jnp.einsum('bhts,bhse->bhte', a, vb))
    return jnp.concatenate(outs, axis=2).transpose(0, 2, 1, 3)


def head_groupnorm(y, g, b):
    B, S, H, e = y.shape
    yf = y.astype(jnp.float32)
    mu = jnp.mean(yf, axis=-1, keepdims=True)
    var = jnp.mean(jnp.square(yf - mu), axis=-1, keepdims=True)
    yn = ((yf - mu) * lax.rsqrt(var + EPS)).reshape(B, S, H * e)
    return (yn * g.astype(jnp.float32) + b.astype(jnp.float32)).astype(y.dtype)


def head_rmsnorm(y, g):
    B, S, H, e = y.shape
    yf = y.astype(jnp.float32)
    yn = (yf * lax.rsqrt(jnp.mean(yf * yf, axis=-1, keepdims=True) + EPS)).reshape(B, S, H * e)
    return (yn * g.astype(jnp.float32)).astype(y.dtype)


def setup_inputs(seed: int = 0) -> dict:
    key = jax.random.key(seed)
    ks = jax.random.split(key, 9)
    f32 = jnp.float32
    x = jax.random.normal(ks[0], (BATCH, SEQ, D_MODEL), f32)
    norm_gain = 1.0 + 0.02 * jax.random.normal(ks[1], (DEPTH, D_MODEL), f32)
    w_in = jax.random.normal(ks[2], (DEPTH, D_MODEL, IN_COLS), f32) * D_MODEL ** -0.5
    ret_gn_gain = 1.0 + 0.02 * jax.random.normal(ks[3], (DEPTH, RET_WIDTH), f32)
    ret_gn_bias = 0.02 * jax.random.normal(ks[4], (DEPTH, RET_WIDTH), f32)
    sb_norm_gain = 1.0 + 0.02 * jax.random.normal(ks[5], (DEPTH, SB_WIDTH), f32)
    w_out = jax.random.normal(ks[6], (DEPTH, D_MIX, D_MODEL), f32) * D_MIX ** -0.5
    final_norm_gain = 1.0 + 0.02 * jax.random.normal(ks[7], (D_MODEL,), f32)
    return {"x": x, "norm_gain": norm_gain, "w_in": w_in, "ret_gn_gain": ret_gn_gain,
            "ret_gn_bias": ret_gn_bias, "sb_norm_gain": sb_norm_gain, "w_out": w_out,
            "final_norm_gain": final_norm_gain}


def reference(x, norm_gain, w_in, ret_gn_gain, ret_gn_bias, sb_norm_gain, w_out, final_norm_gain):
    B, S, _ = x.shape
    splits = [RET_WIDTH * i for i in range(1, 5)] + [4 * RET_WIDTH + SB_WIDTH * i for i in range(1, 4)]
    for l in range(DEPTH):
        h = rmsnorm(x, norm_gain[l])
        proj = jnp.einsum('bsd,dc->bsc', h, w_in[l])
        rq, rk, rv, rg, sq, sk, sv, sg = jnp.split(proj, splits, axis=-1)
        heads = lambda t, H: t.reshape(B, S, H, HEAD_DIM)
        ret = retention(rotary(heads(rq, RET_HEADS)), rotary(heads(rk, RET_HEADS)), heads(rv, RET_HEADS))
        ret = jax.nn.silu(rg) * head_groupnorm(ret, ret_gn_gain[l], ret_gn_bias[l])
        sb = stick_breaking(heads(sq, SB_HEADS), heads(sk, SB_HEADS), heads(sv, SB_HEADS))
        sb = jax.nn.silu(sg) * head_rmsnorm(sb, sb_norm_gain[l])
        mix = jnp.concatenate([ret, sb], axis=-1)
        x = x + jnp.einsum('bsc,cd->bsd', mix, w_out[l])
    return rmsnorm(x, final_norm_gain)
```

```python
import functools

import jax
import jax.numpy as jnp
from jax import lax
from jax.experimental import pallas as pl
from jax.experimental.pallas import tpu as pltpu

HEAD_DIM = 128
CHUNK = 128
ROPE_THETA = 10000.0
EPS = 1e-6
N_GROUPS = 8

F32 = jnp.float32
BF16 = jnp.bfloat16

VMEM_LIMIT_BYTES = 56 * 1024 * 1024


def _nt_dot(a, b):
    return lax.dot_general(a, b, (((1,), (1,)), ((), ())), preferred_element_type=F32)


def _tn_dot(a, b):
    return lax.dot_general(a, b, (((0,), (0,)), ((), ())), preferred_element_type=F32)


def _silu(g):
    return g * (1.0 / (1.0 + jnp.exp(-g)))


def _in_proj_kernel(x_ref, gain_ref, w_ref, cos_ref, sin_ref, out_ref, h_ref, *, k_scale):
    j = pl.program_id(1)

    @pl.when(j == 0)
    def _():
        xf = x_ref[...]
        ms = jnp.mean(xf * xf, axis=-1, keepdims=True)
        h_ref[...] = ((xf * lax.rsqrt(ms + EPS)) * gain_ref[...]).astype(BF16)

    acc = jnp.dot(h_ref[...], w_ref[...], preferred_element_type=F32)
    heads_per_tile = out_ref.shape[0]

    @pl.when(j < 2)
    def _():
        cos = cos_ref[...]
        sin = sin_ref[...]
        s = jnp.where(j == 1, k_scale, 1.0).astype(F32)
        for c in range(heads_per_tile):
            blk = acc[:, c * HEAD_DIM:(c + 1) * HEAD_DIM]
            rot = blk * cos + pltpu.roll(blk, HEAD_DIM // 2, 1) * sin
            out_ref[c] = (rot * s).astype(BF16)

    @pl.when(j >= 2)
    def _():
        for c in range(heads_per_tile):
            out_ref[c] = acc[:, c * HEAD_DIM:(c + 1) * HEAD_DIM].astype(BF16)


def _in_proj(x2d, gain, w_bf, cos_t, sin_t, seq, *, tm=1024):
    n_rows, d_model = x2d.shape
    in_cols = w_bf.shape[1]
    tn = in_cols // N_GROUPS
    heads_per_tile = tn // HEAD_DIM
    seq_tiles = seq // tm
    kernel = functools.partial(_in_proj_kernel, k_scale=HEAD_DIM ** -0.5)
    return pl.pallas_call(
        kernel,
        grid=(n_rows // tm, N_GROUPS),
        in_specs=[
            pl.BlockSpec((tm, d_model), lambda i, j: (i, 0)),
            pl.BlockSpec((1, d_model), lambda i, j: (0, 0)),
            pl.BlockSpec((d_model, tn), lambda i, j: (0, j)),
            pl.BlockSpec((tm, HEAD_DIM), lambda i, j: (i % seq_tiles, 0)),
            pl.BlockSpec((tm, HEAD_DIM), lambda i, j: (i % seq_tiles, 0)),
        ],
        out_specs=pl.BlockSpec((heads_per_tile, tm, HEAD_DIM), lambda i, j: (j, i, 0)),
        out_shape=jax.ShapeDtypeStruct((in_cols // HEAD_DIM, n_rows, HEAD_DIM), BF16),
        scratch_shapes=[pltpu.VMEM((tm, d_model), BF16)],
        compiler_params=pltpu.CompilerParams(
            dimension_semantics=("arbitrary", "arbitrary"),
            vmem_limit_bytes=VMEM_LIMIT_BYTES),
        name="in_proj",
    )(x2d, gain, w_bf, cos_t, sin_t)


def _retention_kernel(q_ref, k_ref, v_ref, g_ref, decay_ref, xi_ref, zeta_ref, gc_ref,
                      gn_g_ref, gn_b_ref, out_ref, r_ref):
    @pl.when(pl.program_id(2) == 0)
    def _():
        r_ref[...] = jnp.zeros_like(r_ref)

    decay = decay_ref[0]
    xi = xi_ref[0]
    zeta = zeta_ref[0]
    gamma_c = gc_ref[0]
    gn_g = gn_g_ref[...]
    gn_b = gn_b_ref[...]
    r = r_ref[...]
    n_chunks = q_ref.shape[1] // CHUNK
    for c in range(n_chunks):
        rows = slice(c * CHUNK, (c + 1) * CHUNK)
        qc = q_ref[0, rows, :]
        kc = k_ref[0, rows, :]
        vc = v_ref[0, rows, :]
        scores = _nt_dot(qc, kc) * decay
        intra = jnp.dot(scores.astype(BF16), vc, preferred_element_type=F32)
        kz = (kc.astype(F32) * zeta).astype(BF16)
        kv = _tn_dot(kz, vc)
        qx = (qc.astype(F32) * xi).astype(BF16)
        cross = jnp.dot(qx, r.astype(BF16), preferred_element_type=F32)
        r = gamma_c * r + kv
        o = intra + cross
        mu = jnp.mean(o, axis=-1, keepdims=True)
        d = o - mu
        var = jnp.mean(d * d, axis=-1, keepdims=True)
        y = (d * lax.rsqrt(var + EPS)) * gn_g + gn_b
        gate = g_ref[0, rows, :].astype(F32)
        out_ref[rows, :] = (_silu(gate) * y).astype(BF16)
    r_ref[...] = r


def _retention(qkvg, tables, gn_gain, gn_bias, batch, seq, n_heads, *, ts=1024):
    decay, xi_b, zeta_b, gc_b = tables
    n_rows = batch * seq
    s_tiles = seq // ts
    def slab(group):
        return pl.BlockSpec((1, ts, HEAD_DIM),
                            lambda b, h, s, group=group: (group * n_heads + h, b * s_tiles + s, 0))
    per_head = lambda b, h, s: (h, 0, 0)
    return pl.pallas_call(
        _retention_kernel,
        grid=(batch, n_heads, s_tiles),
        in_specs=[
            slab(0), slab(1), slab(2), slab(3),
            pl.BlockSpec((1, CHUNK, CHUNK), per_head),
            pl.BlockSpec((1, CHUNK, HEAD_DIM), per_head),
            pl.BlockSpec((1, CHUNK, HEAD_DIM), per_head),
            pl.BlockSpec((1, 1, HEAD_DIM), per_head),
            pl.BlockSpec((1, HEAD_DIM), lambda b, h, s: (0, h)),
            pl.BlockSpec((1, HEAD_DIM), lambda b, h, s: (0, h)),
        ],
        out_specs=pl.BlockSpec((ts, HEAD_DIM), lambda b, h, s: (b * s_tiles + s, h)),
        out_shape=jax.ShapeDtypeStruct((n_rows, n_heads * HEAD_DIM), BF16),
        scratch_shapes=[pltpu.VMEM((HEAD_DIM, HEAD_DIM), F32)],
        compiler_params=pltpu.CompilerParams(
            dimension_semantics=("arbitrary", "arbitrary", "arbitrary"),
            vmem_limit_bytes=VMEM_LIMIT_BYTES),
        name="retention",
    )(qkvg, qkvg, qkvg, qkvg, decay, xi_b, zeta_b, gc_b, gn_gain, gn_bias)


def _stickbreak_kernel(q_ref, k_ref, v_ref, g_ref, gain_ref, out_ref, *, tq, scale):
    seq = q_ref.shape[1]
    n_q = seq // tq
    row = lax.broadcasted_iota(jnp.int32, (tq, tq), 0)
    col = lax.broadcasted_iota(jnp.int32, (tq, tq), 1)
    causal = col < row
    tri = (row >= col).astype(BF16)
    gain = gain_ref[...]

    def tile(qi, j, carry, masked):
        kj = k_ref[0, pl.ds(j * tq, tq), :]
        vj = v_ref[0, pl.ds(j * tq, tq), :]
        z = _nt_dot(qi, kj) * scale
        lk = -(jnp.maximum(z, 0.0) + jnp.log(1.0 + jnp.exp(-jnp.abs(z))))
        if masked:
            lk = jnp.where(causal, lk, 0.0)
        hi = lk.astype(BF16)
        lo = (lk - hi.astype(F32)).astype(BF16)
        csum = (jnp.dot(hi, tri, preferred_element_type=F32)
                + jnp.dot(lo, tri, preferred_element_type=F32))
        a = jnp.exp(z + csum + carry)
        if masked:
            a = jnp.where(causal, a, 0.0)
        pv = jnp.dot(a.astype(BF16), vj, preferred_element_type=F32)
        total = jnp.broadcast_to(csum[:, 0:1], csum.shape)
        return pv, total

    def q_block(i, _):
        rows = pl.ds(i * tq, tq)
        qi = q_ref[0, rows, :]
        acc, carry = tile(qi, i, jnp.zeros((tq, tq), F32), True)

        def key_block(jj, state):
            acc, carry = state
            pv, total = tile(qi, i - 1 - jj, carry, False)
            return acc + pv, carry + total

        acc, _ = lax.fori_loop(0, i, key_block, (acc, carry))
        ms = jnp.mean(acc * acc, axis=-1, keepdims=True)
        y = (acc * lax.rsqrt(ms + EPS)) * gain
        gate = g_ref[0, rows, :].astype(F32)
        out_ref[rows, :] = (_silu(gate) * y).astype(BF16)
        return 0

    lax.fori_loop(0, n_q, q_block, 0)


def _stickbreak(qkvg, gain, batch, seq, n_heads, first_group, *, tq=128):
    n_rows = batch * seq
    def slab(group):
        return pl.BlockSpec((1, seq, HEAD_DIM),
                            lambda b, h, group=group: ((first_group + group) * n_heads + h, b, 0))
    kernel = functools.partial(_stickbreak_kernel, tq=tq, scale=HEAD_DIM ** -0.5)
    return pl.pallas_call(
        kernel,
        grid=(batch, n_heads),
        in_specs=[slab(0), slab(1), slab(2), slab(3),
                  pl.BlockSpec((1, HEAD_DIM), lambda b, h: (0, h))],
        out_specs=pl.BlockSpec((seq, HEAD_DIM), lambda b, h: (b, h)),
        out_shape=jax.ShapeDtypeStruct((n_rows, n_heads * HEAD_DIM), BF16),
        compiler_params=pltpu.CompilerParams(
            dimension_semantics=("arbitrary", "arbitrary"),
            vmem_limit_bytes=VMEM_LIMIT_BYTES),
        name="stickbreak",
    )(qkvg, qkvg, qkvg, qkvg, gain)


def _out_proj_kernel(ret_ref, sb_ref, x_ref, w_ret_ref, w_sb_ref, fgain_ref, out_ref, *, final_norm):
    y = (jnp.dot(ret_ref[...], w_ret_ref[...], preferred_element_type=F32)
         + jnp.dot(sb_ref[...], w_sb_ref[...], preferred_element_type=F32))
    xn = x_ref[...] + y
    if final_norm:
        ms = jnp.mean(xn * xn, axis=-1, keepdims=True)
        xn = (xn * lax.rsqrt(ms + EPS)) * fgain_ref[...]
    out_ref[...] = xn


def _out_proj(mix_ret, mix_sb, x2d, w_ret, w_sb, fgain, final_norm, *, tm=512):
    n_rows, d_model = x2d.shape
    ret_w = mix_ret.shape[1]
    sb_w = mix_sb.shape[1]
    kernel = functools.partial(_out_proj_kernel, final_norm=final_norm)
    return pl.pallas_call(
        kernel,
        grid=(n_rows // tm,),
        in_specs=[
            pl.BlockSpec((tm, ret_w), lambda i: (i, 0)),
            pl.BlockSpec((tm, sb_w), lambda i: (i, 0)),
            pl.BlockSpec((tm, d_model), lambda i: (i, 0)),
            pl.BlockSpec((ret_w, d_model), lambda i: (0, 0)),
            pl.BlockSpec((sb_w, d_model), lambda i: (0, 0)),
            pl.BlockSpec((1, d_model), lambda i: (0, 0)),
        ],
        out_specs=pl.BlockSpec((tm, d_model), lambda i: (i, 0)),
        out_shape=jax.ShapeDtypeStruct((n_rows, d_model), F32),
        compiler_params=pltpu.CompilerParams(
            dimension_semantics=("arbitrary",),
            vmem_limit_bytes=VMEM_LIMIT_BYTES),
        name="out_proj",
    )(mix_ret, mix_sb, x2d, w_ret, w_sb, fgain)


def _rotary_tables(seq):
    half = HEAD_DIM // 2
    inv = ROPE_THETA ** (-jnp.arange(half, dtype=F32) / half)
    ang = jnp.arange(seq, dtype=F32)[:, None] * inv[None, :]
    cos = jnp.cos(ang)
    sin = jnp.sin(ang)
    return jnp.concatenate([cos, cos], axis=-1), jnp.concatenate([-sin, sin], axis=-1)


def _retention_tables(n_heads):
    lg = jnp.log1p(-jnp.exp2(-5.0 - jnp.arange(n_heads, dtype=F32)))
    n = jnp.arange(CHUNK, dtype=F32)
    rel = n[:, None] - n[None, :]
    decay = jnp.where(rel >= 0, jnp.exp(lg[:, None, None] * jnp.maximum(rel, 0.0)), 0.0)
    xi = jnp.exp(lg[:, None] * (n + 1.0))
    zeta = jnp.exp(lg[:, None] * (CHUNK - 1.0 - n))
    gamma_c = jnp.exp(lg * CHUNK)
    xi_b = jnp.broadcast_to(xi[:, :, None], (n_heads, CHUNK, HEAD_DIM))
    zeta_b = jnp.broadcast_to(zeta[:, :, None], (n_heads, CHUNK, HEAD_DIM))
    gc_b = jnp.broadcast_to(gamma_c[:, None, None], (n_heads, 1, HEAD_DIM))
    return decay, xi_b, zeta_b, gc_b


def kernel(x, norm_gain, w_in, ret_gn_gain, ret_gn_bias, sb_norm_gain, w_out, final_norm_gain):
    batch, seq, d_model = x.shape
    depth = norm_gain.shape[0]
    ret_width = ret_gn_gain.shape[1]
    sb_width = sb_norm_gain.shape[1]
    ret_heads = ret_width // HEAD_DIM
    sb_heads = sb_width // HEAD_DIM
    assert ret_heads == sb_heads and w_in.shape[2] == N_GROUPS * ret_width

    cos_t, sin_t = _rotary_tables(seq)
    tables = _retention_tables(ret_heads)
    fgain = final_norm_gain.reshape(1, d_model)
    x2d = x.reshape(batch * seq, d_model)
    for l in range(depth):
        qkvg = _in_proj(x2d, norm_gain[l].reshape(1, d_model), w_in[l].astype(BF16), cos_t, sin_t, seq)
        mix_ret = _retention(qkvg, tables, ret_gn_gain[l].reshape(1, ret_width),
                             ret_gn_bias[l].reshape(1, ret_width), batch, seq, ret_heads)
        mix_sb = _stickbreak(qkvg, sb_norm_gain[l].reshape(1, sb_width), batch, seq, sb_heads, 4)
        w_out_bf = w_out[l].astype(BF16)
        x2d = _out_proj(mix_ret, mix_sb, x2d, w_out_bf[:ret_width], w_out_bf[ret_width:], fgain,
                        final_norm=(l == depth - 1))
    return x2d.reshape(batch, seq, d_model)
```

```python
import functools

import jax
import jax.numpy as jnp
import numpy as np
from jax import lax
from jax.experimental import pallas as pl
from jax.experimental.pallas import tpu as pltpu

HEAD_DIM = 128
CHUNK = 128
ROPE_THETA = 10000.0
EPS = 1e-6
LOG2E = 1.4426950408889634
N_GROUPS = 8

F32 = jnp.float32
BF16 = jnp.bfloat16

VMEM_LIMIT_BYTES = 56 * 1024 * 1024


def _nt_dot(a, b):
    return lax.dot_general(a, b, (((1,), (1,)), ((), ())), preferred_element_type=F32)


def _tn_dot(a, b):
    return lax.dot_general(a, b, (((0,), (0,)), ((), ())), preferred_element_type=F32)


def _silu(g):
    return g * (1.0 / (1.0 + jnp.exp(-g)))


def _in_proj_kernel(x_ref, gain_ref, w_ref, cos_ref, sin_ref, out_ref, h_ref, *, k_scale):
    j = pl.program_id(1)

    @pl.when(j == 0)
    def _():
        xf = x_ref[...]
        ms = jnp.mean(xf * xf, axis=-1, keepdims=True)
        h_ref[...] = ((xf * lax.rsqrt(ms + EPS)) * gain_ref[...]).astype(BF16)

    acc = jnp.dot(h_ref[...], w_ref[...], preferred_element_type=F32)
    heads_per_tile = out_ref.shape[0]

    @pl.when(j < 2)
    def _():
        cos = cos_ref[...]
        sin = sin_ref[...]
        s = jnp.where(j == 1, k_scale, 1.0).astype(F32)
        for c in range(heads_per_tile):
            blk = acc[:, c * HEAD_DIM:(c + 1) * HEAD_DIM]
            rot = blk * cos + pltpu.roll(blk, HEAD_DIM // 2, 1) * sin
            out_ref[c] = (rot * s).astype(BF16)

    @pl.when(j >= 2)
    def _():
        for c in range(heads_per_tile):
            out_ref[c] = acc[:, c * HEAD_DIM:(c + 1) * HEAD_DIM].astype(BF16)


def _in_proj(x2d, gain, w_bf, cos_t, sin_t, seq, *, tm=1024):
    n_rows, d_model = x2d.shape
    in_cols = w_bf.shape[1]
    tn = in_cols // N_GROUPS
    heads_per_tile = tn // HEAD_DIM
    seq_tiles = seq // tm
    kernel = functools.partial(_in_proj_kernel, k_scale=HEAD_DIM ** -0.5)
    return pl.pallas_call(
        kernel,
        grid=(n_rows // tm, N_GROUPS),
        in_specs=[
            pl.BlockSpec((tm, d_model), lambda i, j: (i, 0)),
            pl.BlockSpec((1, d_model), lambda i, j: (0, 0)),
            pl.BlockSpec((d_model, tn), lambda i, j: (0, j)),
            pl.BlockSpec((tm, HEAD_DIM), lambda i, j: (i % seq_tiles, 0)),
            pl.BlockSpec((tm, HEAD_DIM), lambda i, j: (i % seq_tiles, 0)),
        ],
        out_specs=pl.BlockSpec((heads_per_tile, tm, HEAD_DIM), lambda i, j: (j, i, 0)),
        out_shape=jax.ShapeDtypeStruct((in_cols // HEAD_DIM, n_rows, HEAD_DIM), BF16),
        scratch_shapes=[pltpu.VMEM((tm, d_model), BF16)],
        compiler_params=pltpu.CompilerParams(
            dimension_semantics=("arbitrary", "arbitrary"),
            vmem_limit_bytes=VMEM_LIMIT_BYTES),
        name="in_proj",
    )(x2d, gain, w_bf, cos_t, sin_t)


def _retention_kernel(q_ref, k_ref, v_ref, g_ref, decay_ref, xi_ref, zeta_ref, gc_ref,
                      gn_g_ref, gn_b_ref, out_ref, r_ref):
    @pl.when(pl.program_id(2) == 0)
    def _():
        r_ref[...] = jnp.zeros_like(r_ref)

    decay = decay_ref[0]
    xi = xi_ref[0]
    zeta = zeta_ref[0]
    gamma_c = gc_ref[0]
    gn_g = gn_g_ref[...]
    gn_b = gn_b_ref[...]
    r = r_ref[...]
    n_chunks = q_ref.shape[1] // CHUNK
    for c in range(n_chunks):
        rows = slice(c * CHUNK, (c + 1) * CHUNK)
        qc = q_ref[0, rows, :]
        kc = k_ref[0, rows, :]
        vc = v_ref[0, rows, :]
        scores = _nt_dot(qc, kc) * decay
        intra = jnp.dot(scores.astype(BF16), vc, preferred_element_type=F32)
        kz = (kc.astype(F32) * zeta).astype(BF16)
        kv = _tn_dot(kz, vc)
        qx = (qc.astype(F32) * xi).astype(BF16)
        cross = jnp.dot(qx, r.astype(BF16), preferred_element_type=F32)
        r = gamma_c * r + kv
        o = intra + cross
        mu = jnp.mean(o, axis=-1, keepdims=True)
        d = o - mu
        var = jnp.mean(d * d, axis=-1, keepdims=True)
        y = (d * lax.rsqrt(var + EPS)) * gn_g + gn_b
        gate = g_ref[0, rows, :].astype(F32)
        out_ref[rows, :] = (_silu(gate) * y).astype(BF16)
    r_ref[...] = r


def _retention(qkvg, tables, gn_gain, gn_bias, batch, seq, n_heads, *, ts=1024):
    decay, xi_b, zeta_b, gc_b = tables
    n_rows = batch * seq
    s_tiles = seq // ts
    def slab(group):
        return pl.BlockSpec((1, ts, HEAD_DIM),
                            lambda b, h, s, group=group: (group * n_heads + h, b * s_tiles + s, 0))
    per_head = lambda b, h, s: (h, 0, 0)
    return pl.pallas_call(
        _retention_kernel,
        grid=(batch, n_heads, s_tiles),
        in_specs=[
            slab(0), slab(1), slab(2), slab(3),
            pl.BlockSpec((1, CHUNK, CHUNK), per_head),
            pl.BlockSpec((1, CHUNK, HEAD_DIM), per_head),
            pl.BlockSpec((1, CHUNK, HEAD_DIM), per_head),
            pl.BlockSpec((1, 1, HEAD_DIM), per_head),
            pl.BlockSpec((1, HEAD_DIM), lambda b, h, s: (0, h)),
            pl.BlockSpec((1, HEAD_DIM), lambda b, h, s: (0, h)),
        ],
        out_specs=pl.BlockSpec((ts, HEAD_DIM), lambda b, h, s: (b * s_tiles + s, h)),
        out_shape=jax.ShapeDtypeStruct((n_rows, n_heads * HEAD_DIM), BF16),
        scratch_shapes=[pltpu.VMEM((HEAD_DIM, HEAD_DIM), F32)],
        compiler_params=pltpu.CompilerParams(
            dimension_semantics=("arbitrary", "arbitrary", "arbitrary"),
            vmem_limit_bytes=VMEM_LIMIT_BYTES),
        name="retention",
    )(qkvg, qkvg, qkvg, qkvg, decay, xi_b, zeta_b, gc_b, gn_gain, gn_bias)


def _sb_tile_schedule(n_blk, width):
    remaining = {i: i for i in range(n_blk)}
    q_idx, k_idx = [], []
    while any(remaining.values()):
        group = sorted((i for i in remaining if remaining[i] > 0),
                       key=lambda i: (-remaining[i], -i))[:width]
        assert len(group) == width
        for i in group:
            remaining[i] -= 1
            q_idx.append(i)
            k_idx.append(remaining[i])
    return np.asarray(q_idx, np.int32), np.asarray(k_idx, np.int32)


def _stickbreak_kernel(q_tab, k_tab, q_ref, k_ref, v_ref, g_ref, gain_ref, out_ref,
                       acc_ref, carry_ref, tri_ref, *, blk, width, scale):
    seq = q_ref.shape[1]
    n_blk = seq // blk
    n_groups = q_tab.shape[0] // width
    lane_reps = blk // HEAD_DIM
    row = lax.broadcasted_iota(jnp.int32, (blk, blk), 0)
    col = lax.broadcasted_iota(jnp.int32, (blk, blk), 1)
    causal = col < row
    tri_ref[...] = (row >= col).astype(BF16)

    def rows_of(i):
        return pl.ds(pl.multiple_of(i * blk, blk), blk)

    def scores(i, j, masked):
        z = _nt_dot(q_ref[0, rows_of(i), :], k_ref[0, rows_of(j), :]) * scale
        sp = jnp.maximum(z, 0.0) + jnp.log(1.0 + jnp.exp2(jnp.abs(z) * (-LOG2E)))
        if masked:
            sp = jnp.where(causal, sp, 0.0)
        hi = sp.astype(BF16)
        lo = (sp - hi.astype(F32)).astype(BF16)
        tri = tri_ref[...]
        csum = (jnp.dot(hi, tri, preferred_element_type=F32)
                + jnp.dot(lo, tri, preferred_element_type=F32))
        return z, csum

    def weights(z, csum, carry, masked):
        log_a = z - csum
        if carry is not None:
            log_a = log_a - jnp.concatenate([carry] * lane_reps, axis=1)
        a = jnp.exp(log_a)
        if masked:
            a = jnp.where(causal, a, 0.0)
        total = jnp.broadcast_to(csum[:, 0:1], (blk, HEAD_DIM))
        return a.astype(BF16), total

    def diagonal_group(n, _):
        for t in range(width):
            i = n * width + t
            z, csum = scores(i, i, True)
            a, total = weights(z, csum, None, True)
            acc_ref[rows_of(i), :] = jnp.dot(a, v_ref[0, rows_of(i), :], preferred_element_type=F32)
            carry_ref[rows_of(i), :] = total
        return 0

    def off_diagonal_group(n, _):
        qs = [q_tab[n * width + t] for t in range(width)]
        ks = [k_tab[n * width + t] for t in range(width)]
        zc = [scores(i, j, False) for i, j in zip(qs, ks)]
        carries = [carry_ref[rows_of(i), :] for i in qs]
        at = [weights(z, csum, carry, False) for (z, csum), carry in zip(zc, carries)]
        pvs = [jnp.dot(a, v_ref[0, rows_of(j), :], preferred_element_type=F32)
               for (a, _), j in zip(at, ks)]
        accs = [acc_ref[rows_of(i), :] for i in qs]
        for i, carry, (_, total), acc, pv in zip(qs, carries, at, accs, pvs):
            carry_ref[rows_of(i), :] = carry + total
            acc_ref[rows_of(i), :] = acc + pv
        return 0

    def finish(i, _):
        acc = acc_ref[rows_of(i), :]
        ms = jnp.mean(acc * acc, axis=-1, keepdims=True)
        y = (acc * lax.rsqrt(ms + EPS)) * gain_ref[...]
        gate = g_ref[0, rows_of(i), :].astype(F32)
        out_ref[rows_of(i), :] = (_silu(gate) * y).astype(BF16)
        return 0

    lax.fori_loop(0, n_blk // width, diagonal_group, 0)
    lax.fori_loop(0, n_groups, off_diagonal_group, 0)
    lax.fori_loop(0, n_blk, finish, 0)


def _stickbreak(qkvg, gain, batch, seq, n_heads, first_group, *, blk=256, width=4):
    n_rows = batch * seq
    n_blk = seq // blk
    assert n_blk % width == 0
    q_tab, k_tab = _sb_tile_schedule(n_blk, width)
    def slab(group):
        return pl.BlockSpec((1, seq, HEAD_DIM),
                            lambda b, h, qt, kt, group=group: ((first_group + group) * n_heads + h, b, 0))
    kernel = functools.partial(_stickbreak_kernel, blk=blk, width=width, scale=HEAD_DIM ** -0.5)
    grid_spec = pltpu.PrefetchScalarGridSpec(
        num_scalar_prefetch=2,
        grid=(batch, n_heads),
        in_specs=[slab(0), slab(1), slab(2), slab(3),
                  pl.BlockSpec((1, HEAD_DIM), lambda b, h, qt, kt: (0, h))],
        out_specs=pl.BlockSpec((seq, HEAD_DIM), lambda b, h, qt, kt: (b, h)),
        scratch_shapes=[pltpu.VMEM((seq, HEAD_DIM), F32),
                        pltpu.VMEM((seq, HEAD_DIM), F32),
                        pltpu.VMEM((blk, blk), BF16)],
    )
    return pl.pallas_call(
        kernel,
        grid_spec=grid_spec,
        out_shape=jax.ShapeDtypeStruct((n_rows, n_heads * HEAD_DIM), BF16),
        compiler_params=pltpu.CompilerParams(
            dimension_semantics=("arbitrary", "arbitrary"),
            vmem_limit_bytes=VMEM_LIMIT_BYTES),
        name="stickbreak",
    )(jnp.asarray(q_tab), jnp.asarray(k_tab), qkvg, qkvg, qkvg, qkvg, gain)


def _out_proj_kernel(ret_ref, sb_ref, x_ref, w_ret_ref, w_sb_ref, fgain_ref, out_ref, *, final_norm):
    y = (jnp.dot(ret_ref[...], w_ret_ref[...], preferred_element_type=F32)
         + jnp.dot(sb_ref[...], w_sb_ref[...], preferred_element_type=F32))
    xn = x_ref[...] + y
    if final_norm:
        ms = jnp.mean(xn * xn, axis=-1, keepdims=True)
        xn = (xn * lax.rsqrt(ms + EPS)) * fgain_ref[...]
    out_ref[...] = xn


def _out_proj(mix_ret, mix_sb, x2d, w_ret, w_sb, fgain, final_norm, *, tm=512):
    n_rows, d_model = x2d.shape
    ret_w = mix_ret.shape[1]
    sb_w = mix_sb.shape[1]
    kernel = functools.partial(_out_proj_kernel, final_norm=final_norm)
    return pl.pallas_call(
        kernel,
        grid=(n_rows // tm,),
        in_specs=[
            pl.BlockSpec((tm, ret_w), lambda i: (i, 0)),
            pl.BlockSpec((tm, sb_w), lambda i: (i, 0)),
            pl.BlockSpec((tm, d_model), lambda i: (i, 0)),
            pl.BlockSpec((ret_w, d_model), lambda i: (0, 0)),
            pl.BlockSpec((sb_w, d_model), lambda i: (0, 0)),
            pl.BlockSpec((1, d_model), lambda i: (0, 0)),
        ],
        out_specs=pl.BlockSpec((tm, d_model), lambda i: (i, 0)),
        out_shape=jax.ShapeDtypeStruct((n_rows, d_model), F32),
        compiler_params=pltpu.CompilerParams(
            dimension_semantics=("arbitrary",),
            vmem_limit_bytes=VMEM_LIMIT_BYTES),
        name="out_proj",
    )(mix_ret, mix_sb, x2d, w_ret, w_sb, fgain)


def _rotary_tables(seq):
    half = HEAD_DIM // 2
    inv = ROPE_THETA ** (-jnp.arange(half, dtype=F32) / half)
    ang = jnp.arange(seq, dtype=F32)[:, None] * inv[None, :]
    cos = jnp.cos(ang)
    sin = jnp.sin(ang)
    return jnp.concatenate([cos, cos], axis=-1), jnp.concatenate([-sin, sin], axis=-1)


def _retention_tables(n_heads):
    lg = jnp.log1p(-jnp.exp2(-5.0 - jnp.arange(n_heads, dtype=F32)))
    n = jnp.arange(CHUNK, dtype=F32)
    rel = n[:, None] - n[None, :]
    decay = jnp.where(rel >= 0, jnp.exp(lg[:, None, None] * jnp.maximum(rel, 0.0)), 0.0)
    xi = jnp.exp(lg[:, None] * (n + 1.0))
    zeta = jnp.exp(lg[:, None] * (CHUNK - 1.0 - n))
    gamma_c = jnp.exp(lg * CHUNK)
    xi_b = jnp.broadcast_to(xi[:, :, None], (n_heads, CHUNK, HEAD_DIM))
    zeta_b = jnp.broadcast_to(zeta[:, :, None], (n_heads, CHUNK, HEAD_DIM))
    gc_b = jnp.broadcast_to(gamma_c[:, None, None], (n_heads, 1, HEAD_DIM))
    return decay, xi_b, zeta_b, gc_b


def kernel(x, norm_gain, w_in, ret_gn_gain, ret_gn_bias, sb_norm_gain, w_out, final_norm_gain):
    batch, seq, d_model = x.shape
    depth = norm_gain.shape[0]
    ret_width = ret_gn_gain.shape[1]
    sb_width = sb_norm_gain.shape[1]
    ret_heads = ret_width // HEAD_DIM
    sb_heads = sb_width // HEAD_DIM
    assert ret_heads == sb_heads and w_in.shape[2] == N_GROUPS * ret_width

    cos_t, sin_t = _rotary_tables(seq)
    tables = _retention_tables(ret_heads)
    fgain = final_norm_gain.reshape(1, d_model)
    x2d = x.reshape(batch * seq, d_model)
    for l in range(depth):
        qkvg = _in_proj(x2d, norm_gain[l].reshape(1, d_model), w_in[l].astype(BF16), cos_t, sin_t, seq)
        mix_ret = _retention(qkvg, tables, ret_gn_gain[l].reshape(1, ret_width),
                             ret_gn_bias[l].reshape(1, ret_width), batch, seq, ret_heads)
        mix_sb = _stickbreak(qkvg, sb_norm_gain[l].reshape(1, sb_width), batch, seq, sb_heads, 4)
        w_out_bf = w_out[l].astype(BF16)
        x2d = _out_proj(mix_ret, mix_sb, x2d, w_out_bf[:ret_width], w_out_bf[ret_width:], fgain,
                        final_norm=(l == depth - 1))
    return x2d.reshape(batch, seq, d_model)
```

```python
import functools

import jax
import jax.numpy as jnp
import numpy as np
from jax import lax
from jax.experimental import pallas as pl
from jax.experimental.pallas import tpu as pltpu

HEAD_DIM = 128
CHUNK = 128
ROPE_THETA = 10000.0
EPS = 1e-6
LOG2E = 1.4426950408889634
MASKED_LOGIT = -1e30
UNDERFLOW_LOG = 105.0
N_GROUPS = 8

F32 = jnp.float32
BF16 = jnp.bfloat16

VMEM_LIMIT_BYTES = 56 * 1024 * 1024


def _nt_dot(a, b):
    return lax.dot_general(a, b, (((1,), (1,)), ((), ())), preferred_element_type=F32)


def _tn_dot(a, b):
    return lax.dot_general(a, b, (((0,), (0,)), ((), ())), preferred_element_type=F32)


def _silu(g):
    return g * (1.0 / (1.0 + jnp.exp(-g)))


def _in_proj_kernel(kind_tab, x_ref, gain_ref, w_ref, cos_ref, sin_ref, out_ref, h_ref):
    @pl.when(pl.program_id(1) == 0)
    def _():
        xf = x_ref[...]
        ms = jnp.mean(xf * xf, axis=-1, keepdims=True)
        h_ref[...] = ((xf * lax.rsqrt(ms + EPS)) * gain_ref[...]).astype(BF16)

    acc = jnp.dot(h_ref[...], w_ref[...], preferred_element_type=F32)
    cos = cos_ref[0]
    sin = sin_ref[0]
    for c in range(out_ref.shape[0]):
        blk = acc[:, c * HEAD_DIM:(c + 1) * HEAD_DIM]
        out_ref[c] = (blk * cos + pltpu.roll(blk, HEAD_DIM // 2, 1) * sin).astype(BF16)


EPILOGUE_ROTARY, EPILOGUE_ROTARY_SCALED, EPILOGUE_SCALED, EPILOGUE_IDENTITY = range(4)
GROUP_EPILOGUE = (EPILOGUE_ROTARY, EPILOGUE_ROTARY_SCALED, EPILOGUE_IDENTITY, EPILOGUE_IDENTITY,
                  EPILOGUE_SCALED, EPILOGUE_IDENTITY, EPILOGUE_IDENTITY, EPILOGUE_IDENTITY)


def _epilogue_tables(seq):
    half = HEAD_DIM // 2
    inv = ROPE_THETA ** (-jnp.arange(half, dtype=F32) / half)
    ang = jnp.arange(seq, dtype=F32)[:, None] * inv[None, :]
    cos = jnp.concatenate([jnp.cos(ang)] * 2, axis=-1)
    sin = jnp.concatenate([-jnp.sin(ang), jnp.sin(ang)], axis=-1)
    ones = jnp.ones_like(cos)
    zeros = jnp.zeros_like(sin)
    scale = HEAD_DIM ** -0.5
    return (jnp.stack([cos, cos * scale, ones * scale, ones]),
            jnp.stack([sin, sin * scale, zeros, zeros]))


def _in_proj(x2d, gain, w_bf, cos_t, sin_t, seq, *, tm=1024):
    n_rows, d_model = x2d.shape
    in_cols = w_bf.shape[1]
    tn = in_cols // N_GROUPS
    heads_per_tile = tn // HEAD_DIM
    seq_tiles = seq // tm
    table = pl.BlockSpec((1, tm, HEAD_DIM), lambda i, j, kind: (kind[j], i % seq_tiles, 0))
    grid_spec = pltpu.PrefetchScalarGridSpec(
        num_scalar_prefetch=1,
        grid=(n_rows // tm, N_GROUPS),
        in_specs=[
            pl.BlockSpec((tm, d_model), lambda i, j, kind: (i, 0)),
            pl.BlockSpec((1, d_model), lambda i, j, kind: (0, 0)),
            pl.BlockSpec((d_model, tn), lambda i, j, kind: (0, j)),
            table, table,
        ],
        out_specs=pl.BlockSpec((heads_per_tile, tm, HEAD_DIM), lambda i, j, kind: (j, i, 0)),
        scratch_shapes=[pltpu.VMEM((tm, d_model), BF16)],
    )
    return pl.pallas_call(
        _in_proj_kernel,
        grid_spec=grid_spec,
        out_shape=jax.ShapeDtypeStruct((in_cols // HEAD_DIM, n_rows, HEAD_DIM), BF16),
        compiler_params=pltpu.CompilerParams(
            dimension_semantics=("arbitrary", "arbitrary"),
            vmem_limit_bytes=VMEM_LIMIT_BYTES),
        name="in_proj",
    )(jnp.asarray(GROUP_EPILOGUE, jnp.int32), x2d, gain, w_bf, cos_t, sin_t)


def _retention_kernel(q_ref, k_ref, v_ref, g_ref, decay_ref, xi_ref, zeta_ref, gc_ref,
                      gn_g_ref, gn_b_ref, out_ref, r_ref):
    @pl.when(pl.program_id(2) == 0)
    def _():
        r_ref[...] = jnp.zeros_like(r_ref)

    decay = decay_ref[0]
    xi = xi_ref[0]
    zeta = zeta_ref[0]
    gamma_c = gc_ref[0]
    gn_g = gn_g_ref[...]
    gn_b = gn_b_ref[...]
    r = r_ref[...]
    n_chunks = q_ref.shape[1] // CHUNK
    for c in range(n_chunks):
        rows = slice(c * CHUNK, (c + 1) * CHUNK)
        qc = q_ref[0, rows, :]
        kc = k_ref[0, rows, :]
        vc = v_ref[0, rows, :]
        scores = _nt_dot(qc, kc) * decay
        intra = jnp.dot(scores.astype(BF16), vc, preferred_element_type=F32)
        kz = (kc.astype(F32) * zeta).astype(BF16)
        kv = _tn_dot(kz, vc)
        qx = (qc.astype(F32) * xi).astype(BF16)
        cross = jnp.dot(qx, r.astype(BF16), preferred_element_type=F32)
        r = gamma_c * r + kv
        o = intra + cross
        mu = jnp.mean(o, axis=-1, keepdims=True)
        d = o - mu
        var = jnp.mean(d * d, axis=-1, keepdims=True)
        y = (d * lax.rsqrt(var + EPS)) * gn_g + gn_b
        gate = g_ref[0, rows, :].astype(F32)
        out_ref[rows, :] = (_silu(gate) * y).astype(BF16)
    r_ref[...] = r


def _retention(qkvg, tables, gn_gain, gn_bias, batch, seq, n_heads, *, ts=1024):
    decay, xi_b, zeta_b, gc_b = tables
    n_rows = batch * seq
    s_tiles = seq // ts
    def slab(group):
        return pl.BlockSpec((1, ts, HEAD_DIM),
                            lambda b, h, s, group=group: (group * n_heads + h, b * s_tiles + s, 0))
    per_head = lambda b, h, s: (h, 0, 0)
    return pl.pallas_call(
        _retention_kernel,
        grid=(batch, n_heads, s_tiles),
        in_specs=[
            slab(0), slab(1), slab(2), slab(3),
            pl.BlockSpec((1, CHUNK, CHUNK), per_head),
            pl.BlockSpec((1, CHUNK, HEAD_DIM), per_head),
            pl.BlockSpec((1, CHUNK, HEAD_DIM), per_head),
            pl.BlockSpec((1, 1, HEAD_DIM), per_head),
            pl.BlockSpec((1, HEAD_DIM), lambda b, h, s: (0, h)),
            pl.BlockSpec((1, HEAD_DIM), lambda b, h, s: (0, h)),
        ],
        out_specs=pl.BlockSpec((ts, HEAD_DIM), lambda b, h, s: (b * s_tiles + s, h)),
        out_shape=jax.ShapeDtypeStruct((n_rows, n_heads * HEAD_DIM), BF16),
        scratch_shapes=[pltpu.VMEM((HEAD_DIM, HEAD_DIM), F32)],
        compiler_params=pltpu.CompilerParams(
            dimension_semantics=("arbitrary", "arbitrary", "arbitrary"),
            vmem_limit_bytes=VMEM_LIMIT_BYTES),
        name="retention",
    )(qkvg, qkvg, qkvg, qkvg, decay, xi_b, zeta_b, gc_b, gn_gain, gn_bias)


def _sb_near_tiles(n_blk, width):
    assert n_blk % width == 0
    q_idx = np.concatenate([np.arange(n_blk), np.arange(n_blk)])
    k_idx = np.concatenate([np.arange(n_blk), np.maximum(np.arange(n_blk) - 1, 0)])
    valid = np.concatenate([np.ones(n_blk), np.arange(n_blk) >= 1])
    return q_idx.astype(np.int32), k_idx.astype(np.int32), valid.astype(np.int32)


def _stickbreak_kernel(q_tab, k_tab, valid_tab, q_ref, k_ref, v_ref, g_ref, gain_ref, out_ref,
                       acc_ref, carry_ref, tri_ref, bias_ref, z_a, hi_a, lo_a, z_b, hi_b, lo_b,
                       *, blk, width):
    seq = q_ref.shape[1]
    n_blk = seq // blk
    n_near_groups = q_tab.shape[0] // width
    lane_reps = blk // HEAD_DIM
    row = lax.broadcasted_iota(jnp.int32, (blk, blk), 0)
    col = lax.broadcasted_iota(jnp.int32, (blk, blk), 1)
    tri_ref[...] = (row >= col).astype(BF16)
    bias_ref[0] = jnp.zeros((blk, blk), F32)
    bias_ref[1] = jnp.where(col < row, 0.0, MASKED_LOGIT)
    acc_ref[...] = jnp.zeros_like(acc_ref)
    carry_ref[...] = jnp.zeros_like(carry_ref)

    def rows_of(i):
        return pl.ds(pl.multiple_of(i * blk, blk), blk)

    def scores(qs, ks, z_buf, hi_buf, lo_buf):
        for t, (i, j) in enumerate(zip(qs, ks)):
            on_diagonal = (i == j).astype(jnp.int32)
            z = _nt_dot(q_ref[0, rows_of(i), :], k_ref[0, rows_of(j), :]) + bias_ref[on_diagonal]
            sp = jnp.maximum(z, 0.0) + jnp.log(1.0 + jnp.exp2(jnp.abs(z) * (-LOG2E)))
            hi = sp.astype(BF16)
            z_buf[t] = z
            hi_buf[t] = hi
            lo_buf[t] = (sp - hi.astype(F32)).astype(BF16)

    def weights(qs, ks, valid, z_buf, hi_buf, lo_buf):
        tri = tri_ref[...]
        csums = [jnp.dot(hi_buf[t], tri, preferred_element_type=F32)
                 + jnp.dot(lo_buf[t], tri, preferred_element_type=F32) for t in range(width)]
        carries = [carry_ref[rows_of(i), :] for i in qs]
        a = [jnp.exp(z_buf[t] - csums[t] - jnp.concatenate([carries[t]] * lane_reps, axis=1))
             for t in range(width)]
        pvs = [jnp.dot(a[t].astype(BF16), v_ref[0, rows_of(ks[t]), :], preferred_element_type=F32)
               for t in range(width)]
        accs = [acc_ref[rows_of(i), :] for i in qs]
        for t, i in enumerate(qs):
            w = valid[t].astype(F32)
            total = jnp.broadcast_to(csums[t][:, 0:1], (blk, HEAD_DIM))
            carry_ref[rows_of(i), :] = carries[t] + w * total
            acc_ref[rows_of(i), :] = accs[t] + w * pvs[t]

    buf_a = (z_a, hi_a, lo_a)
    buf_b = (z_b, hi_b, lo_b)

    def near_tiles(g):
        return ([q_tab[g * width + t] for t in range(width)],
                [k_tab[g * width + t] for t in range(width)],
                [valid_tab[g * width + t] for t in range(width)])

    def near_scores(g, buf):
        qs, ks, _ = near_tiles(g)
        scores(qs, ks, *buf)

    def near_weights(g, buf):
        weights(*near_tiles(g), *buf)

    def two_near_groups(m, _):
        g = 2 * m
        near_scores(g + 1, buf_b)
        near_weights(g, buf_a)
        near_scores(g + 2, buf_a)
        near_weights(g + 1, buf_b)
        return 0

    near_scores(0, buf_a)
    lax.fori_loop(0, n_near_groups // 2 - 1, two_near_groups, 0)
    near_scores(n_near_groups - 1, buf_b)
    near_weights(n_near_groups - 2, buf_a)
    near_weights(n_near_groups - 1, buf_b)

    def group_live(n, d):
        live = jnp.bool_(False)
        for t in range(width):
            i = n * width + t
            live = live | ((i >= d) & (jnp.min(carry_ref[rows_of(i), :]) < UNDERFLOW_LOG))
        return live

    def any_live(d):
        live = jnp.bool_(False)
        for n in range(n_blk // width):
            live = live | group_live(n, d)
        return live

    def far_wave(d):
        for n in range(n_blk // width):
            @pl.when(group_live(n, d))
            def _():
                qs = [n * width + t for t in range(width)]
                ks = [jnp.maximum(i - d, 0) for i in qs]
                scores(qs, ks, *buf_a)
                weights(qs, ks, [i >= d for i in qs], *buf_a)
        return d + 1

    lax.while_loop(lambda d: (d < n_blk) & any_live(d), far_wave, jnp.int32(2))

    def finish(i, _):
        acc = acc_ref[rows_of(i), :]
        ms = jnp.mean(acc * acc, axis=-1, keepdims=True)
        y = (acc * lax.rsqrt(ms + EPS)) * gain_ref[...]
        gate = g_ref[0, rows_of(i), :].astype(F32)
        out_ref[rows_of(i), :] = (_silu(gate) * y).astype(BF16)
        return 0

    lax.fori_loop(0, n_blk, finish, 0)


def _stickbreak(qkvg, gain, batch, seq, n_heads, first_group, *, blk=256, width=4):
    n_rows = batch * seq
    n_blk = seq // blk
    q_tab, k_tab, valid_tab = _sb_near_tiles(n_blk, width)
    assert (q_tab.shape[0] // width) % 2 == 0
    def slab(group):
        return pl.BlockSpec((1, seq, HEAD_DIM),
                            lambda b, h, *_, group=group: ((first_group + group) * n_heads + h, b, 0))
    kernel = functools.partial(_stickbreak_kernel, blk=blk, width=width)
    stage_buffers = [pltpu.VMEM((width, blk, blk), F32),
                     pltpu.VMEM((width, blk, blk), BF16),
                     pltpu.VMEM((width, blk, blk), BF16)]
    grid_spec = pltpu.PrefetchScalarGridSpec(
        num_scalar_prefetch=3,
        grid=(batch, n_heads),
        in_specs=[slab(0), slab(1), slab(2), slab(3),
                  pl.BlockSpec((1, HEAD_DIM), lambda b, h, *_: (0, h))],
        out_specs=pl.BlockSpec((seq, HEAD_DIM), lambda b, h, *_: (b, h)),
        scratch_shapes=[pltpu.VMEM((seq, HEAD_DIM), F32),
                        pltpu.VMEM((seq, HEAD_DIM), F32),
                        pltpu.VMEM((blk, blk), BF16),
                        pltpu.VMEM((2, blk, blk), F32),
                        ] + stage_buffers + stage_buffers,
    )
    return pl.pallas_call(
        kernel,
        grid_spec=grid_spec,
        out_shape=jax.ShapeDtypeStruct((n_rows, n_heads * HEAD_DIM), BF16),
        compiler_params=pltpu.CompilerParams(
            dimension_semantics=("arbitrary", "arbitrary"),
            vmem_limit_bytes=VMEM_LIMIT_BYTES),
        name="stickbreak",
    )(jnp.asarray(q_tab), jnp.asarray(k_tab), jnp.asarray(valid_tab), qkvg, qkvg, qkvg, qkvg, gain)


def _out_proj_kernel(ret_ref, sb_ref, x_ref, w_ret_ref, w_sb_ref, fgain_ref, out_ref, *, final_norm):
    y = (jnp.dot(ret_ref[...], w_ret_ref[...], preferred_element_type=F32)
         + jnp.dot(sb_ref[...], w_sb_ref[...], preferred_element_type=F32))
    xn = x_ref[...] + y
    if final_norm:
        ms = jnp.mean(xn * xn, axis=-1, keepdims=True)
        xn = (xn * lax.rsqrt(ms + EPS)) * fgain_ref[...]
    out_ref[...] = xn


def _out_proj(mix_ret, mix_sb, x2d, w_ret, w_sb, fgain, final_norm, *, tm=512):
    n_rows, d_model = x2d.shape
    ret_w = mix_ret.shape[1]
    sb_w = mix_sb.shape[1]
    kernel = functools.partial(_out_proj_kernel, final_norm=final_norm)
    return pl.pallas_call(
        kernel,
        grid=(n_rows // tm,),
        in_specs=[
            pl.BlockSpec((tm, ret_w), lambda i: (i, 0)),
            pl.BlockSpec((tm, sb_w), lambda i: (i, 0)),
            pl.BlockSpec((tm, d_model), lambda i: (i, 0)),
            pl.BlockSpec((ret_w, d_model), lambda i: (0, 0)),
            pl.BlockSpec((sb_w, d_model), lambda i: (0, 0)),
            pl.BlockSpec((1, d_model), lambda i: (0, 0)),
        ],
        out_specs=pl.BlockSpec((tm, d_model), lambda i: (i, 0)),
        out_shape=jax.ShapeDtypeStruct((n_rows, d_model), F32),
        compiler_params=pltpu.CompilerParams(
            dimension_semantics=("arbitrary",),
            vmem_limit_bytes=VMEM_LIMIT_BYTES),
        name="out_proj",
    )(mix_ret, mix_sb, x2d, w_ret, w_sb, fgain)


def _retention_tables(n_heads):
    lg = jnp.log1p(-jnp.exp2(-5.0 - jnp.arange(n_heads, dtype=F32)))
    n = jnp.arange(CHUNK, dtype=F32)
    rel = n[:, None] - n[None, :]
    decay = jnp.where(rel >= 0, jnp.exp(lg[:, None, None] * jnp.maximum(rel, 0.0)), 0.0)
    xi = jnp.exp(lg[:, None] * (n + 1.0))
    zeta = jnp.exp(lg[:, None] * (CHUNK - 1.0 - n))
    gamma_c = jnp.exp(lg * CHUNK)
    xi_b = jnp.broadcast_to(xi[:, :, None], (n_heads, CHUNK, HEAD_DIM))
    zeta_b = jnp.broadcast_to(zeta[:, :, None], (n_heads, CHUNK, HEAD_DIM))
    gc_b = jnp.broadcast_to(gamma_c[:, None, None], (n_heads, 1, HEAD_DIM))
    return decay, xi_b, zeta_b, gc_b


def kernel(x, norm_gain, w_in, ret_gn_gain, ret_gn_bias, sb_norm_gain, w_out, final_norm_gain):
    batch, seq, d_model = x.shape
    depth = norm_gain.shape[0]
    ret_width = ret_gn_gain.shape[1]
    sb_width = sb_norm_gain.shape[1]
    ret_heads = ret_width // HEAD_DIM
    sb_heads = sb_width // HEAD_DIM
    assert ret_heads == sb_heads and w_in.shape[2] == N_GROUPS * ret_width

    cos_t, sin_t = _epilogue_tables(seq)
    tables = _retention_tables(ret_heads)
    fgain = final_norm_gain.reshape(1, d_model)
    x2d = x.reshape(batch * seq, d_model)
    for l in range(depth):
        qkvg = _in_proj(x2d, norm_gain[l].reshape(1, d_model), w_in[l].astype(BF16), cos_t, sin_t, seq)
        mix_ret = _retention(qkvg, tables, ret_gn_gain[l].reshape(1, ret_width),
                             ret_gn_bias[l].reshape(1, ret_width), batch, seq, ret_heads)
        mix_sb = _stickbreak(qkvg, sb_norm_gain[l].reshape(1, sb_width), batch, seq, sb_heads, 4)
        w_out_bf = w_out[l].astype(BF16)
        x2d = _out_proj(mix_ret, mix_sb, x2d, w_out_bf[:ret_width], w_out_bf[ret_width:], fgain,
                        final_norm=(l == depth - 1))
    return x2d.reshape(batch, seq, d_model)
```

```python
import functools

import jax
import jax.numpy as jnp
from jax import lax
from jax.experimental import pallas as pl
from jax.experimental.pallas import tpu as pltpu

HEAD_DIM = 128
CHUNK = 128
ROPE_THETA = 10000.0
EPS = 1e-6
LOG2E = 1.4426950408889634
MASKED_LOGIT = -1e30
UNDERFLOW_LOG = 105.0
N_GROUPS = 8
IN_PROJ_ROW_SLAB = 256

F32 = jnp.float32
BF16 = jnp.bfloat16

VMEM_LIMIT_BYTES = 56 * 1024 * 1024


def _nt_dot(a, b):
    return lax.dot_general(a, b, (((1,), (1,)), ((), ())), preferred_element_type=F32)


def _tn_dot(a, b):
    return lax.dot_general(a, b, (((0,), (0,)), ((), ())), preferred_element_type=F32)


def _silu(g):
    return g * (1.0 / (1.0 + jnp.exp(-g)))


def _in_proj_kernel(kind_tab, x_ref, gain_ref, w_ref, cos_ref, sin_ref, out_ref, h_ref):
    @pl.when(pl.program_id(1) == 0)
    def _():
        xf = x_ref[...]
        ms = jnp.mean(xf * xf, axis=-1, keepdims=True)
        h_ref[...] = ((xf * lax.rsqrt(ms + EPS)) * gain_ref[...]).astype(BF16)

    w = w_ref[...]
    for r in range(0, h_ref.shape[0], IN_PROJ_ROW_SLAB):
        rows = slice(r, r + IN_PROJ_ROW_SLAB)
        acc = jnp.dot(h_ref[rows, :], w, preferred_element_type=F32)
        cos = cos_ref[0, rows, :]
        sin = sin_ref[0, rows, :]
        for c in range(out_ref.shape[0]):
            blk = acc[:, c * HEAD_DIM:(c + 1) * HEAD_DIM]
            out_ref[c, rows, :] = (blk * cos + pltpu.roll(blk, HEAD_DIM // 2, 1) * sin).astype(BF16)


EPILOGUE_ROTARY, EPILOGUE_ROTARY_SCALED, EPILOGUE_SCALED, EPILOGUE_IDENTITY = range(4)
GROUP_EPILOGUE = (EPILOGUE_ROTARY, EPILOGUE_ROTARY_SCALED, EPILOGUE_IDENTITY, EPILOGUE_IDENTITY,
                  EPILOGUE_SCALED, EPILOGUE_IDENTITY, EPILOGUE_IDENTITY, EPILOGUE_IDENTITY)


def _epilogue_tables(seq):
    half = HEAD_DIM // 2
    inv = ROPE_THETA ** (-jnp.arange(half, dtype=F32) / half)
    ang = jnp.arange(seq, dtype=F32)[:, None] * inv[None, :]
    cos = jnp.concatenate([jnp.cos(ang)] * 2, axis=-1)
    sin = jnp.concatenate([-jnp.sin(ang), jnp.sin(ang)], axis=-1)
    ones = jnp.ones_like(cos)
    zeros = jnp.zeros_like(sin)
    scale = HEAD_DIM ** -0.5
    return (jnp.stack([cos, cos * scale, ones * scale, ones]),
            jnp.stack([sin, sin * scale, zeros, zeros]))


def _in_proj(x2d, gain, w_bf, cos_t, sin_t, seq, *, tm=1024):
    n_rows, d_model = x2d.shape
    in_cols = w_bf.shape[1]
    tn = in_cols // N_GROUPS
    heads_per_tile = tn // HEAD_DIM
    seq_tiles = seq // tm
    assert tm % IN_PROJ_ROW_SLAB == 0
    table = pl.BlockSpec((1, tm, HEAD_DIM), lambda i, j, kind: (kind[j], i % seq_tiles, 0))
    grid_spec = pltpu.PrefetchScalarGridSpec(
        num_scalar_prefetch=1,
        grid=(n_rows // tm, N_GROUPS),
        in_specs=[
            pl.BlockSpec((tm, d_model), lambda i, j, kind: (i, 0)),
            pl.BlockSpec((1, d_model), lambda i, j, kind: (0, 0)),
            pl.BlockSpec((d_model, tn), lambda i, j, kind: (0, j)),
            table, table,
        ],
        out_specs=pl.BlockSpec((heads_per_tile, tm, HEAD_DIM), lambda i, j, kind: (j, i, 0)),
        scratch_shapes=[pltpu.VMEM((tm, d_model), BF16)],
    )
    return pl.pallas_call(
        _in_proj_kernel,
        grid_spec=grid_spec,
        out_shape=jax.ShapeDtypeStruct((in_cols // HEAD_DIM, n_rows, HEAD_DIM), BF16),
        compiler_params=pltpu.CompilerParams(
            dimension_semantics=("arbitrary", "arbitrary"),
            vmem_limit_bytes=VMEM_LIMIT_BYTES),
        name="in_proj",
    )(jnp.asarray(GROUP_EPILOGUE, jnp.int32), x2d, gain, w_bf, cos_t, sin_t)


def _retention_kernel(q_ref, k_ref, v_ref, g_ref, decay_ref, xi_ref, zeta_ref, gc_ref,
                      gn_g_ref, gn_b_ref, out_ref, r_ref):
    @pl.when(pl.program_id(2) == 0)
    def _():
        r_ref[...] = jnp.zeros_like(r_ref)

    decay = decay_ref[0]
    xi = xi_ref[0]
    zeta = zeta_ref[0]
    gamma_c = gc_ref[0]
    gn_g = gn_g_ref[...]
    gn_b = gn_b_ref[...]
    r = r_ref[...]
    n_chunks = q_ref.shape[1] // CHUNK
    for c in range(n_chunks):
        rows = slice(c * CHUNK, (c + 1) * CHUNK)
        qc = q_ref[0, rows, :]
        kc = k_ref[0, rows, :]
        vc = v_ref[0, rows, :]
        scores = _nt_dot(qc, kc) * decay
        intra = jnp.dot(scores.astype(BF16), vc, preferred_element_type=F32)
        kz = (kc.astype(F32) * zeta).astype(BF16)
        kv = _tn_dot(kz, vc)
        qx = (qc.astype(F32) * xi).astype(BF16)
        cross = jnp.dot(qx, r.astype(BF16), preferred_element_type=F32)
        r = gamma_c * r + kv
        o = intra + cross
        mu = jnp.mean(o, axis=-1, keepdims=True)
        d = o - mu
        var = jnp.mean(d * d, axis=-1, keepdims=True)
        y = (d * lax.rsqrt(var + EPS)) * gn_g + gn_b
        gate = g_ref[0, rows, :].astype(F32)
        out_ref[rows, :] = (_silu(gate) * y).astype(BF16)
    r_ref[...] = r


def _retention(qkvg, tables, gn_gain, gn_bias, batch, seq, n_heads, *, ts=4096):
    decay, xi_b, zeta_b, gc_b = tables
    n_rows = batch * seq
    s_tiles = seq // ts
    def slab(group):
        return pl.BlockSpec((1, ts, HEAD_DIM),
                            lambda b, h, s, group=group: (group * n_heads + h, b * s_tiles + s, 0))
    per_head = lambda b, h, s: (h, 0, 0)
    return pl.pallas_call(
        _retention_kernel,
        grid=(batch, n_heads, s_tiles),
        in_specs=[
            slab(0), slab(1), slab(2), slab(3),
            pl.BlockSpec((1, CHUNK, CHUNK), per_head),
            pl.BlockSpec((1, CHUNK, HEAD_DIM), per_head),
            pl.BlockSpec((1, CHUNK, HEAD_DIM), per_head),
            pl.BlockSpec((1, 1, HEAD_DIM), per_head),
            pl.BlockSpec((1, HEAD_DIM), lambda b, h, s: (0, h)),
            pl.BlockSpec((1, HEAD_DIM), lambda b, h, s: (0, h)),
        ],
        out_specs=pl.BlockSpec((ts, HEAD_DIM), lambda b, h, s: (b * s_tiles + s, h)),
        out_shape=jax.ShapeDtypeStruct((n_rows, n_heads * HEAD_DIM), BF16),
        scratch_shapes=[pltpu.VMEM((HEAD_DIM, HEAD_DIM), F32)],
        compiler_params=pltpu.CompilerParams(
            dimension_semantics=("arbitrary", "arbitrary", "arbitrary"),
            vmem_limit_bytes=VMEM_LIMIT_BYTES),
        name="retention",
    )(qkvg, qkvg, qkvg, qkvg, decay, xi_b, zeta_b, gc_b, gn_gain, gn_bias)


def _stickbreak_kernel(q_ref, k_ref, v_ref, g_ref, gain_ref, out_ref,
                       acc_ref, carry_ref, tri_ref, bias_ref, z_a, sp_a, z_b, sp_b, *, blk, width):
    seq = q_ref.shape[1]
    n_blk = seq // blk
    n_groups = n_blk // width
    lane_reps = blk // HEAD_DIM
    row = lax.broadcasted_iota(jnp.int32, (blk, blk), 0)
    col = lax.broadcasted_iota(jnp.int32, (blk, blk), 1)
    tri_ref[...] = (row >= col).astype(BF16)
    bias_ref[...] = jnp.where(col < row, 0.0, MASKED_LOGIT)

    def rows_of(i):
        return pl.ds(i * blk if isinstance(i, int) else pl.multiple_of(i * blk, blk), blk)

    def scores(tiles, z_buf, sp_buf):
        for t, (i, j, diagonal, _) in enumerate(tiles):
            z = _nt_dot(q_ref[0, rows_of(i), :], k_ref[0, rows_of(j), :])
            if diagonal:
                z = z + bias_ref[...]
            sp = jnp.maximum(z, 0.0) + jnp.log(1.0 + jnp.exp2(jnp.abs(z) * (-LOG2E)))
            z_buf[t] = z
            sp_buf[t] = sp.astype(BF16)

    def weights(tiles, z_buf, sp_buf, first):
        tri = tri_ref[...]
        csums = [jnp.dot(sp_buf[t], tri, preferred_element_type=F32) for t in range(len(tiles))]
        if not first:
            carries = [carry_ref[rows_of(i), :] for i, _, _, _ in tiles]
            csums_c = [c + jnp.concatenate([carry] * lane_reps, axis=1) for c, carry in zip(csums, carries)]
        else:
            csums_c = csums
        pvs = [jnp.dot(jnp.exp(z_buf[t] - csums_c[t]).astype(BF16), v_ref[0, rows_of(j), :],
                       preferred_element_type=F32) for t, (_, j, _, _) in enumerate(tiles)]
        if not first:
            accs = [acc_ref[rows_of(i), :] for i, _, _, _ in tiles]
        for t, (i, _, _, weight) in enumerate(tiles):
            total = jnp.broadcast_to(csums[t][:, 0:1], (blk, HEAD_DIM))
            pv = pvs[t]
            if weight is not None:
                total = weight * total
                pv = weight * pv
            carry_ref[rows_of(i), :] = total if first else carries[t] + total
            acc_ref[rows_of(i), :] = pv if first else accs[t] + pv

    buffers = ((z_a, sp_a), (z_b, sp_b))

    near = [[(i, i - d, d == 0, None) for i in range(n * width, (n + 1) * width) if i >= d]
            for d in (0, 1) for n in range(n_groups)]
    scores(near[0], *buffers[0])
    for g, tiles in enumerate(near):
        if g + 1 < len(near):
            scores(near[g + 1], *buffers[(g + 1) % 2])
        weights(tiles, *buffers[g % 2], first=(g < n_groups))

    def group_live(n, d):
        live = jnp.bool_(False)
        for i in range(n * width, (n + 1) * width):
            live = live | ((i >= d) & (jnp.min(carry_ref[rows_of(i), :]) < UNDERFLOW_LOG))
        return live

    def any_live(d):
        live = jnp.bool_(False)
        for n in range(n_groups):
            live = live | group_live(n, d)
        return live

    def far_wave(d):
        for n in range(n_groups):
            @pl.when(group_live(n, d))
            def _():
                tiles = [(i, jnp.maximum(i - d, 0), False, (i >= d).astype(F32))
                         for i in range(n * width, (n + 1) * width)]
                scores(tiles, *buffers[0])
                weights(tiles, *buffers[0], first=False)
        return d + 1

    lax.while_loop(lambda d: (d < n_blk) & any_live(d), far_wave, jnp.int32(2))

    def finish(i, _):
        acc = acc_ref[rows_of(i), :]
        ms = jnp.mean(acc * acc, axis=-1, keepdims=True)
        y = (acc * lax.rsqrt(ms + EPS)) * gain_ref[...]
        gate = g_ref[0, rows_of(i), :].astype(F32)
        out_ref[rows_of(i), :] = (_silu(gate) * y).astype(BF16)
        return 0

    lax.fori_loop(0, n_blk, finish, 0, unroll=4)


def _stickbreak(qkvg, gain, batch, seq, n_heads, first_group, *, blk=256, width=4):
    n_rows = batch * seq
    assert (seq // blk) % width == 0
    def slab(group):
        return pl.BlockSpec((1, seq, HEAD_DIM),
                            lambda b, h, group=group: ((first_group + group) * n_heads + h, b, 0))
    kernel = functools.partial(_stickbreak_kernel, blk=blk, width=width)
    stage_buffers = [pltpu.VMEM((width, blk, blk), F32),
                     pltpu.VMEM((width, blk, blk), BF16)]
    return pl.pallas_call(
        kernel,
        grid=(batch, n_heads),
        in_specs=[slab(0), slab(1), slab(2), slab(3),
                  pl.BlockSpec((1, HEAD_DIM), lambda b, h: (0, h))],
        out_specs=pl.BlockSpec((seq, HEAD_DIM), lambda b, h: (b, h)),
        out_shape=jax.ShapeDtypeStruct((n_rows, n_heads * HEAD_DIM), BF16),
        scratch_shapes=[pltpu.VMEM((seq, HEAD_DIM), F32),
                        pltpu.VMEM((seq, HEAD_DIM), F32),
                        pltpu.VMEM((blk, blk), BF16),
                        pltpu.VMEM((blk, blk), F32),
                        ] + stage_buffers + stage_buffers,
        compiler_params=pltpu.CompilerParams(
            dimension_semantics=("arbitrary", "arbitrary"),
            vmem_limit_bytes=VMEM_LIMIT_BYTES),
        name="stickbreak",
    )(qkvg, qkvg, qkvg, qkvg, gain)


def _out_proj_kernel(ret_ref, sb_ref, x_ref, w_ret_ref, w_sb_ref, fgain_ref, out_ref, *, final_norm):
    y = (jnp.dot(ret_ref[...], w_ret_ref[...], preferred_element_type=F32)
         + jnp.dot(sb_ref[...], w_sb_ref[...], preferred_element_type=F32))
    xn = x_ref[...] + y
    if final_norm:
        ms = jnp.mean(xn * xn, axis=-1, keepdims=True)
        xn = (xn * lax.rsqrt(ms + EPS)) * fgain_ref[...]
    out_ref[...] = xn


def _out_proj(mix_ret, mix_sb, x2d, w_ret, w_sb, fgain, final_norm, *, tm=512):
    n_rows, d_model = x2d.shape
    ret_w = mix_ret.shape[1]
    sb_w = mix_sb.shape[1]
    kernel = functools.partial(_out_proj_kernel, final_norm=final_norm)
    return pl.pallas_call(
        kernel,
        grid=(n_rows // tm,),
        in_specs=[
            pl.BlockSpec((tm, ret_w), lambda i: (i, 0)),
            pl.BlockSpec((tm, sb_w), lambda i: (i, 0)),
            pl.BlockSpec((tm, d_model), lambda i: (i, 0)),
            pl.BlockSpec((ret_w, d_model), lambda i: (0, 0)),
            pl.BlockSpec((sb_w, d_model), lambda i: (0, 0)),
            pl.BlockSpec((1, d_model), lambda i: (0, 0)),
        ],
        out_specs=pl.BlockSpec((tm, d_model), lambda i: (i, 0)),
        out_shape=jax.ShapeDtypeStruct((n_rows, d_model), F32),
        compiler_params=pltpu.CompilerParams(
            dimension_semantics=("arbitrary",),
            vmem_limit_bytes=VMEM_LIMIT_BYTES),
        name="out_proj",
    )(mix_ret, mix_sb, x2d, w_ret, w_sb, fgain)


def _retention_tables(n_heads):
    lg = jnp.log1p(-jnp.exp2(-5.0 - jnp.arange(n_heads, dtype=F32)))
    n = jnp.arange(CHUNK, dtype=F32)
    rel = n[:, None] - n[None, :]
    decay = jnp.where(rel >= 0, jnp.exp(lg[:, None, None] * jnp.maximum(rel, 0.0)), 0.0)
    xi = jnp.exp(lg[:, None] * (n + 1.0))
    zeta = jnp.exp(lg[:, None] * (CHUNK - 1.0 - n))
    gamma_c = jnp.exp(lg * CHUNK)
    xi_b = jnp.broadcast_to(xi[:, :, None], (n_heads, CHUNK, HEAD_DIM))
    zeta_b = jnp.broadcast_to(zeta[:, :, None], (n_heads, CHUNK, HEAD_DIM))
    gc_b = jnp.broadcast_to(gamma_c[:, None, None], (n_heads, 1, HEAD_DIM))
    return decay, xi_b, zeta_b, gc_b


def kernel(x, norm_gain, w_in, ret_gn_gain, ret_gn_bias, sb_norm_gain, w_out, final_norm_gain):
    batch, seq, d_model = x.shape
    depth = norm_gain.shape[0]
    ret_width = ret_gn_gain.shape[1]
    sb_width = sb_norm_gain.shape[1]
    ret_heads = ret_width // HEAD_DIM
    sb_heads = sb_width // HEAD_DIM
    assert ret_heads == sb_heads and w_in.shape[2] == N_GROUPS * ret_width

    cos_t, sin_t = _epilogue_tables(seq)
    tables = _retention_tables(ret_heads)
    fgain = final_norm_gain.reshape(1, d_model)
    x2d = x.reshape(batch * seq, d_model)
    for l in range(depth):
        qkvg = _in_proj(x2d, norm_gain[l].reshape(1, d_model), w_in[l].astype(BF16), cos_t, sin_t, seq)
        mix_ret = _retention(qkvg, tables, ret_gn_gain[l].reshape(1, ret_width),
                             ret_gn_bias[l].reshape(1, ret_width), batch, seq, ret_heads)
        mix_sb = _stickbreak(qkvg, sb_norm_gain[l].reshape(1, sb_width), batch, seq, sb_heads, 4)
        w_out_bf = w_out[l].astype(BF16)
        x2d = _out_proj(mix_ret, mix_sb, x2d, w_out_bf[:ret_width], w_out_bf[ret_width:], fgain,
                        final_norm=(l == depth - 1))
    return x2d.reshape(batch, seq, d_model)
```

```python
import functools

import jax
import jax.numpy as jnp
from jax import lax
from jax.experimental import pallas as pl
from jax.experimental.pallas import tpu as pltpu

HEAD_DIM = 128
CHUNK = 128
ROPE_THETA = 10000.0
EPS = 1e-6
LOG2E = 1.4426950408889634
MASKED_LOGIT = -1e30
UNDERFLOW_LOG = 105.0
N_GROUPS = 8
IN_PROJ_ROW_SLAB = 256

F32 = jnp.float32
BF16 = jnp.bfloat16

VMEM_LIMIT_BYTES = 60 * 1024 * 1024


def _nt_dot(a, b):
    return lax.dot_general(a, b, (((1,), (1,)), ((), ())), preferred_element_type=F32)


def _tn_dot(a, b):
    return lax.dot_general(a, b, (((0,), (0,)), ((), ())), preferred_element_type=F32)


def _silu(g):
    return (0.5 * g) * (1.0 + jnp.tanh(0.5 * g))


def _in_proj_kernel(kind_tab, x_ref, gain_ref, w_ref, *refs):
    n_tables = (len(refs) - 2) // 2
    cos_refs, sin_refs = refs[:n_tables], refs[n_tables:2 * n_tables]
    out_ref, h_ref = refs[2 * n_tables:]
    heads_per_group = out_ref.shape[0] // n_tables

    @pl.when(pl.program_id(1) == 0)
    def _():
        xf = x_ref[...]
        ms = jnp.mean(xf * xf, axis=-1, keepdims=True)
        h_ref[...] = ((xf * lax.rsqrt(ms + EPS)) * gain_ref[...]).astype(BF16)

    w = w_ref[...]
    for r in range(0, h_ref.shape[0], IN_PROJ_ROW_SLAB):
        rows = slice(r, r + IN_PROJ_ROW_SLAB)
        acc = jnp.dot(h_ref[rows, :], w, preferred_element_type=F32)
        for c in range(out_ref.shape[0]):
            cos = cos_refs[c // heads_per_group][0, rows, :]
            sin = sin_refs[c // heads_per_group][0, rows, :]
            blk = acc[:, c * HEAD_DIM:(c + 1) * HEAD_DIM]
            out_ref[c, rows, :] = (blk * cos + pltpu.roll(blk, HEAD_DIM // 2, 1) * sin).astype(BF16)


EPILOGUE_ROTARY, EPILOGUE_ROTARY_SCALED, EPILOGUE_SCALED, EPILOGUE_IDENTITY = range(4)
GROUP_EPILOGUE = (EPILOGUE_ROTARY, EPILOGUE_ROTARY_SCALED, EPILOGUE_IDENTITY, EPILOGUE_IDENTITY,
                  EPILOGUE_SCALED, EPILOGUE_IDENTITY, EPILOGUE_IDENTITY, EPILOGUE_IDENTITY)


def _epilogue_tables(seq):
    half = HEAD_DIM // 2
    inv = ROPE_THETA ** (-jnp.arange(half, dtype=F32) / half)
    ang = jnp.arange(seq, dtype=F32)[:, None] * inv[None, :]
    cos = jnp.concatenate([jnp.cos(ang)] * 2, axis=-1)
    sin = jnp.concatenate([-jnp.sin(ang), jnp.sin(ang)], axis=-1)
    ones = jnp.ones_like(cos)
    zeros = jnp.zeros_like(sin)
    scale = HEAD_DIM ** -0.5
    return (jnp.stack([cos, cos * scale, ones * scale, ones]),
            jnp.stack([sin, sin * scale, zeros, zeros]))


def _in_proj(x2d, gain, w_bf, cos_t, sin_t, seq, *, tm=1024, groups_per_tile=2):
    n_rows, d_model = x2d.shape
    in_cols = w_bf.shape[1]
    tn = groups_per_tile * (in_cols // N_GROUPS)
    heads_per_tile = tn // HEAD_DIM
    seq_tiles = seq // tm
    assert tm % IN_PROJ_ROW_SLAB == 0 and N_GROUPS % groups_per_tile == 0
    def table(g):
        return pl.BlockSpec((1, tm, HEAD_DIM),
                            lambda i, j, kind: (kind[j * groups_per_tile + g], i % seq_tiles, 0))
    tables = [table(g) for g in range(groups_per_tile)]
    grid_spec = pltpu.PrefetchScalarGridSpec(
        num_scalar_prefetch=1,
        grid=(n_rows // tm, N_GROUPS // groups_per_tile),
        in_specs=[
            pl.BlockSpec((tm, d_model), lambda i, j, kind: (i, 0)),
            pl.BlockSpec((1, d_model), lambda i, j, kind: (0, 0)),
            pl.BlockSpec((d_model, tn), lambda i, j, kind: (0, j)),
        ] + tables + tables,
        out_specs=pl.BlockSpec((heads_per_tile, tm, HEAD_DIM), lambda i, j, kind: (j, i, 0)),
        scratch_shapes=[pltpu.VMEM((tm, d_model), BF16)],
    )
    return pl.pallas_call(
        _in_proj_kernel,
        grid_spec=grid_spec,
        out_shape=jax.ShapeDtypeStruct((in_cols // HEAD_DIM, n_rows, HEAD_DIM), BF16),
        compiler_params=pltpu.CompilerParams(
            dimension_semantics=("arbitrary", "arbitrary"),
            vmem_limit_bytes=VMEM_LIMIT_BYTES),
        name="in_proj",
    )(jnp.asarray(GROUP_EPILOGUE, jnp.int32), x2d, gain, w_bf,
      *([cos_t] * groups_per_tile), *([sin_t] * groups_per_tile))


def _retention_kernel(q_ref, k_ref, v_ref, g_ref, decay_ref, xi_ref, zeta_ref, gc_ref,
                      gn_g_ref, gn_b_ref, out_ref, r_ref):
    @pl.when(pl.program_id(2) == 0)
    def _():
        r_ref[...] = jnp.zeros_like(r_ref)

    decay = decay_ref[0]
    xi = xi_ref[0]
    zeta = zeta_ref[0]
    gamma_c = gc_ref[0]
    gn_g = gn_g_ref[...]
    gn_b = gn_b_ref[...]
    r = r_ref[...]
    n_chunks = q_ref.shape[1] // CHUNK
    for c in range(n_chunks):
        rows = slice(c * CHUNK, (c + 1) * CHUNK)
        qc = q_ref[0, rows, :]
        kc = k_ref[0, rows, :]
        vc = v_ref[0, rows, :]
        scores = _nt_dot(qc, kc) * decay
        intra = jnp.dot(scores.astype(BF16), vc, preferred_element_type=F32)
        kz = (kc.astype(F32) * zeta).astype(BF16)
        kv = _tn_dot(kz, vc)
        qx = (qc.astype(F32) * xi).astype(BF16)
        cross = jnp.dot(qx, r.astype(BF16), preferred_element_type=F32)
        r = gamma_c * r + kv
        o = intra + cross
        mu = jnp.mean(o, axis=-1, keepdims=True)
        d = o - mu
        var = jnp.mean(d * d, axis=-1, keepdims=True)
        y = (d * lax.rsqrt(var + EPS)) * gn_g + gn_b
        gate = g_ref[0, rows, :].astype(F32)
        out_ref[rows, :] = (_silu(gate) * y).astype(BF16)
    r_ref[...] = r


def _retention(qkvg, tables, gn_gain, gn_bias, batch, seq, n_heads, *, ts=4096):
    decay, xi_b, zeta_b, gc_b = tables
    n_rows = batch * seq
    s_tiles = seq // ts
    def slab(group):
        return pl.BlockSpec((1, ts, HEAD_DIM),
                            lambda b, h, s, group=group: (group * n_heads + h, b * s_tiles + s, 0))
    per_head = lambda b, h, s: (h, 0, 0)
    return pl.pallas_call(
        _retention_kernel,
        grid=(batch, n_heads, s_tiles),
        in_specs=[
            slab(0), slab(1), slab(2), slab(3),
            pl.BlockSpec((1, CHUNK, CHUNK), per_head),
            pl.BlockSpec((1, CHUNK, HEAD_DIM), per_head),
            pl.BlockSpec((1, CHUNK, HEAD_DIM), per_head),
            pl.BlockSpec((1, 1, HEAD_DIM), per_head),
            pl.BlockSpec((1, HEAD_DIM), lambda b, h, s: (0, h)),
            pl.BlockSpec((1, HEAD_DIM), lambda b, h, s: (0, h)),
        ],
        out_specs=pl.BlockSpec((ts, HEAD_DIM), lambda b, h, s: (b * s_tiles + s, h)),
        out_shape=jax.ShapeDtypeStruct((n_rows, n_heads * HEAD_DIM), BF16),
        scratch_shapes=[pltpu.VMEM((HEAD_DIM, HEAD_DIM), F32)],
        compiler_params=pltpu.CompilerParams(
            dimension_semantics=("arbitrary", "arbitrary", "arbitrary"),
            vmem_limit_bytes=VMEM_LIMIT_BYTES),
        name="retention",
    )(qkvg, qkvg, qkvg, qkvg, decay, xi_b, zeta_b, gc_b, gn_gain, gn_bias)


def _stickbreak_kernel(q_ref, k_ref, v_ref, g_ref, gain_ref, out_ref,
                       acc_ref, carry_ref, tri_ref, bias_ref, z_a, sp_a, z_b, sp_b, *, blk, width):
    seq = q_ref.shape[1]
    n_blk = seq // blk
    n_groups = n_blk // width
    lane_reps = blk // HEAD_DIM
    row = lax.broadcasted_iota(jnp.int32, (blk, blk), 0)
    col = lax.broadcasted_iota(jnp.int32, (blk, blk), 1)
    tri_ref[...] = (row >= col).astype(BF16)
    bias_ref[...] = jnp.where(col < row, 0.0, MASKED_LOGIT)

    def rows_of(i):
        return pl.ds(i * blk if isinstance(i, int) else pl.multiple_of(i * blk, blk), blk)

    def scores(tiles, z_buf, sp_buf):
        for t, (i, j, diagonal, _) in enumerate(tiles):
            z = _nt_dot(q_ref[0, rows_of(i), :], k_ref[0, rows_of(j), :])
            if diagonal:
                z = z + bias_ref[...]
            sp = jnp.maximum(z, 0.0) + jnp.log(1.0 + jnp.exp2(jnp.abs(z) * (-LOG2E)))
            z_buf[t] = z
            sp_buf[t] = sp.astype(BF16)

    def weights(tiles, z_buf, sp_buf, first):
        tri = tri_ref[...]
        csums = [jnp.dot(sp_buf[t], tri, preferred_element_type=F32) for t in range(len(tiles))]
        if not first:
            carries = [carry_ref[rows_of(i), :] for i, _, _, _ in tiles]
            csums_c = [c + jnp.concatenate([carry] * lane_reps, axis=1) for c, carry in zip(csums, carries)]
        else:
            csums_c = csums
        pvs = [jnp.dot(jnp.exp(z_buf[t] - csums_c[t]).astype(BF16), v_ref[0, rows_of(j), :],
                       preferred_element_type=F32) for t, (_, j, _, _) in enumerate(tiles)]
        if not first:
            accs = [acc_ref[rows_of(i), :] for i, _, _, _ in tiles]
        for t, (i, _, _, weight) in enumerate(tiles):
            total = jnp.broadcast_to(csums[t][:, 0:1], (blk, HEAD_DIM))
            pv = pvs[t]
            if weight is not None:
                total = weight * total
                pv = weight * pv
            carry_ref[rows_of(i), :] = total if first else carries[t] + total
            acc_ref[rows_of(i), :] = pv if first else accs[t] + pv

    buffers = ((z_a, sp_a), (z_b, sp_b))

    near = [[(i, i - d, d == 0, None) for i in range(n * width, (n + 1) * width) if i >= d]
            for d in (0, 1) for n in range(n_groups)]
    scores(near[0], *buffers[0])
    for g, tiles in enumerate(near):
        if g + 1 < len(near):
            scores(near[g + 1], *buffers[(g + 1) % 2])
        weights(tiles, *buffers[g % 2], first=(g < n_groups))

    def group_live(n, d):
        live = jnp.bool_(False)
        for i in range(n * width, (n + 1) * width):
            live = live | ((i >= d) & (jnp.min(carry_ref[rows_of(i), :]) < UNDERFLOW_LOG))
        return live

    def any_live(d):
        live = jnp.bool_(False)
        for n in range(n_groups):
            live = live | group_live(n, d)
        return live

    def far_wave(d):
        for n in range(n_groups):
            @pl.when(group_live(n, d))
            def _():
                tiles = [(i, jnp.maximum(i - d, 0), False, (i >= d).astype(F32))
                         for i in range(n * width, (n + 1) * width)]
                scores(tiles, *buffers[0])
                weights(tiles, *buffers[0], first=False)
        return d + 1

    lax.while_loop(lambda d: (d < n_blk) & any_live(d), far_wave, jnp.int32(2))

    def finish(i, _):
        acc = acc_ref[rows_of(i), :]
        ms = jnp.mean(acc * acc, axis=-1, keepdims=True)
        y = (acc * lax.rsqrt(ms + EPS)) * gain_ref[...]
        gate = g_ref[0, rows_of(i), :].astype(F32)
        out_ref[rows_of(i), :] = (_silu(gate) * y).astype(BF16)
        return 0

    lax.fori_loop(0, n_blk, finish, 0, unroll=4)


def _stickbreak(qkvg, gain, batch, seq, n_heads, first_group, *, blk=256, width=4):
    n_rows = batch * seq
    assert (seq // blk) % width == 0
    def slab(group):
        return pl.BlockSpec((1, seq, HEAD_DIM),
                            lambda b, h, group=group: ((first_group + group) * n_heads + h, b, 0))
    kernel = functools.partial(_stickbreak_kernel, blk=blk, width=width)
    stage_buffers = [pltpu.VMEM((width, blk, blk), F32),
                     pltpu.VMEM((width, blk, blk), BF16)]
    return pl.pallas_call(
        kernel,
        grid=(batch, n_heads),
        in_specs=[slab(0), slab(1), slab(2), slab(3),
                  pl.BlockSpec((1, HEAD_DIM), lambda b, h: (0, h))],
        out_specs=pl.BlockSpec((seq, HEAD_DIM), lambda b, h: (b, h)),
        out_shape=jax.ShapeDtypeStruct((n_rows, n_heads * HEAD_DIM), BF16),
        scratch_shapes=[pltpu.VMEM((seq, HEAD_DIM), F32),
                        pltpu.VMEM((seq, HEAD_DIM), F32),
                        pltpu.VMEM((blk, blk), BF16),
                        pltpu.VMEM((blk, blk), F32),
                        ] + stage_buffers + stage_buffers,
        compiler_params=pltpu.CompilerParams(
            dimension_semantics=("arbitrary", "arbitrary"),
            vmem_limit_bytes=VMEM_LIMIT_BYTES),
        name="stickbreak",
    )(qkvg, qkvg, qkvg, qkvg, gain)


def _out_proj_kernel(ret_ref, sb_ref, x_ref, w_ret_ref, w_sb_ref, fgain_ref, out_ref, *, final_norm):
    y = (jnp.dot(ret_ref[...], w_ret_ref[...], preferred_element_type=F32)
         + jnp.dot(sb_ref[...], w_sb_ref[...], preferred_element_type=F32))
    xn = x_ref[...] + y
    if final_norm:
        ms = jnp.mean(xn * xn, axis=-1, keepdims=True)
        xn = (xn * lax.rsqrt(ms + EPS)) * fgain_ref[...]
    out_ref[...] = xn


def _out_proj(mix_ret, mix_sb, x2d, w_ret, w_sb, fgain, final_norm, *, tm=512):
    n_rows, d_model = x2d.shape
    ret_w = mix_ret.shape[1]
    sb_w = mix_sb.shape[1]
    kernel = functools.partial(_out_proj_kernel, final_norm=final_norm)
    return pl.pallas_call(
        kernel,
        grid=(n_rows // tm,),
        in_specs=[
            pl.BlockSpec((tm, ret_w), lambda i: (i, 0)),
            pl.BlockSpec((tm, sb_w), lambda i: (i, 0)),
            pl.BlockSpec((tm, d_model), lambda i: (i, 0)),
            pl.BlockSpec((ret_w, d_model), lambda i: (0, 0)),
            pl.BlockSpec((sb_w, d_model), lambda i: (0, 0)),
            pl.BlockSpec((1, d_model), lambda i: (0, 0)),
        ],
        out_specs=pl.BlockSpec((tm, d_model), lambda i: (i, 0)),
        out_shape=jax.ShapeDtypeStruct((n_rows, d_model), F32),
        compiler_params=pltpu.CompilerParams(
            dimension_semantics=("arbitrary",),
            vmem_limit_bytes=VMEM_LIMIT_BYTES),
        name="out_proj",
    )(mix_ret, mix_sb, x2d, w_ret, w_sb, fgain)


def _retention_tables(n_heads):
    lg = jnp.log1p(-jnp.exp2(-5.0 - jnp.arange(n_heads, dtype=F32)))
    n = jnp.arange(CHUNK, dtype=F32)
    rel = n[:, None] - n[None, :]
    decay = jnp.where(rel >= 0, jnp.exp(lg[:, None, None] * jnp.maximum(rel, 0.0)), 0.0)
    xi = jnp.exp(lg[:, None] * (n + 1.0))
    zeta = jnp.exp(lg[:, None] * (CHUNK - 1.0 - n))
    gamma_c = jnp.exp(lg * CHUNK)
    xi_b = jnp.broadcast_to(xi[:, :, None], (n_heads, CHUNK, HEAD_DIM))
    zeta_b = jnp.broadcast_to(zeta[:, :, None], (n_heads, CHUNK, HEAD_DIM))
    gc_b = jnp.broadcast_to(gamma_c[:, None, None], (n_heads, 1, HEAD_DIM))
    return decay, xi_b, zeta_b, gc_b


def kernel(x, norm_gain, w_in, ret_gn_gain, ret_gn_bias, sb_norm_gain, w_out, final_norm_gain):
    batch, seq, d_model = x.shape
    depth = norm_gain.shape[0]
    ret_width = ret_gn_gain.shape[1]
    sb_width = sb_norm_gain.shape[1]
    ret_heads = ret_width // HEAD_DIM
    sb_heads = sb_width // HEAD_DIM
    assert ret_heads == sb_heads and w_in.shape[2] == N_GROUPS * ret_width

    cos_t, sin_t = _epilogue_tables(seq)
    tables = _retention_tables(ret_heads)
    fgain = final_norm_gain.reshape(1, d_model)
    x2d = x.reshape(batch * seq, d_model)
    for l in range(depth):
        qkvg = _in_proj(x2d, norm_gain[l].reshape(1, d_model), w_in[l].astype(BF16), cos_t, sin_t, seq)
        mix_ret = _retention(qkvg, tables, ret_gn_gain[l].reshape(1, ret_width),
                             ret_gn_bias[l].reshape(1, ret_width), batch, seq, ret_heads)
        mix_sb = _stickbreak(qkvg, sb_norm_gain[l].reshape(1, sb_width), batch, seq, sb_heads, 4)
        w_out_bf = w_out[l].astype(BF16)
        x2d = _out_proj(mix_ret, mix_sb, x2d, w_out_bf[:ret_width], w_out_bf[ret_width:], fgain,
                        final_norm=(l == depth - 1))
    return x2d.reshape(batch, seq, d_model)
```

```python
import functools

import jax
import jax.numpy as jnp
from jax import lax
from jax.experimental import pallas as pl
from jax.experimental.pallas import tpu as pltpu

HEAD_DIM = 128
CHUNK = 128
ROPE_THETA = 10000.0
EPS = 1e-6
LOG2E = 1.4426950408889634
MASKED_LOGIT = -1e30
UNDERFLOW_LOG = 105.0
N_GROUPS = 8
IN_PROJ_ROW_SLAB = 256

F32 = jnp.float32
BF16 = jnp.bfloat16

VMEM_LIMIT_BYTES = 60 * 1024 * 1024


def _nt_dot(a, b):
    return lax.dot_general(a, b, (((1,), (1,)), ((), ())), preferred_element_type=F32)


def _tn_dot(a, b):
    return lax.dot_general(a, b, (((0,), (0,)), ((), ())), preferred_element_type=F32)


def _silu(g):
    return (0.5 * g) * (1.0 + jnp.tanh(0.5 * g))


def _in_proj_kernel(kind_tab, x_ref, gain_ref, w_ref, *refs):
    n_tables = (len(refs) - 2) // 2
    cos_refs, sin_refs = refs[:n_tables], refs[n_tables:2 * n_tables]
    out_ref, h_ref = refs[2 * n_tables:]
    heads_per_group = out_ref.shape[0] // n_tables

    @pl.when(pl.program_id(1) == 0)
    def _():
        xf = x_ref[...]
        ms = jnp.mean(xf * xf, axis=-1, keepdims=True)
        h_ref[...] = ((xf * lax.rsqrt(ms + EPS)) * gain_ref[...]).astype(BF16)

    w = w_ref[...]
    for r in range(0, h_ref.shape[0], IN_PROJ_ROW_SLAB):
        rows = slice(r, r + IN_PROJ_ROW_SLAB)
        acc = jnp.dot(h_ref[rows, :], w, preferred_element_type=F32)
        for c in range(out_ref.shape[0]):
            cos = cos_refs[c // heads_per_group][0, rows, :]
            sin = sin_refs[c // heads_per_group][0, rows, :]
            blk = acc[:, c * HEAD_DIM:(c + 1) * HEAD_DIM]
            out_ref[c, rows, :] = (blk * cos + pltpu.roll(blk, HEAD_DIM // 2, 1) * sin).astype(BF16)


EPILOGUE_ROTARY, EPILOGUE_ROTARY_SCALED, EPILOGUE_SCALED, EPILOGUE_IDENTITY = range(4)
GROUP_EPILOGUE = (EPILOGUE_ROTARY, EPILOGUE_ROTARY_SCALED, EPILOGUE_IDENTITY, EPILOGUE_IDENTITY,
                  EPILOGUE_SCALED, EPILOGUE_IDENTITY, EPILOGUE_IDENTITY, EPILOGUE_IDENTITY)


def _epilogue_tables(seq):
    half = HEAD_DIM // 2
    inv = ROPE_THETA ** (-jnp.arange(half, dtype=F32) / half)
    ang = jnp.arange(seq, dtype=F32)[:, None] * inv[None, :]
    cos = jnp.concatenate([jnp.cos(ang)] * 2, axis=-1)
    sin = jnp.concatenate([-jnp.sin(ang), jnp.sin(ang)], axis=-1)
    ones = jnp.ones_like(cos)
    zeros = jnp.zeros_like(sin)
    scale = HEAD_DIM ** -0.5
    return (jnp.stack([cos, cos * scale, ones * scale, ones]),
            jnp.stack([sin, sin * scale, zeros, zeros]))


def _in_proj(x2d, gain, w_bf, cos_t, sin_t, seq, *, tm=1024, groups_per_tile=2):
    n_rows, d_model = x2d.shape
    in_cols = w_bf.shape[1]
    tn = groups_per_tile * (in_cols // N_GROUPS)
    heads_per_tile = tn // HEAD_DIM
    seq_tiles = seq // tm
    assert tm % IN_PROJ_ROW_SLAB == 0 and N_GROUPS % groups_per_tile == 0
    def table(g):
        return pl.BlockSpec((1, tm, HEAD_DIM),
                            lambda i, j, kind: (kind[j * groups_per_tile + g], i % seq_tiles, 0))
    tables = [table(g) for g in range(groups_per_tile)]
    grid_spec = pltpu.PrefetchScalarGridSpec(
        num_scalar_prefetch=1,
        grid=(n_rows // tm, N_GROUPS // groups_per_tile),
        in_specs=[
            pl.BlockSpec((tm, d_model), lambda i, j, kind: (i, 0)),
            pl.BlockSpec((1, d_model), lambda i, j, kind: (0, 0)),
            pl.BlockSpec((d_model, tn), lambda i, j, kind: (0, j)),
        ] + tables + tables,
        out_specs=pl.BlockSpec((heads_per_tile, tm, HEAD_DIM), lambda i, j, kind: (j, i, 0)),
        scratch_shapes=[pltpu.VMEM((tm, d_model), BF16)],
    )
    return pl.pallas_call(
        _in_proj_kernel,
        grid_spec=grid_spec,
        out_shape=jax.ShapeDtypeStruct((in_cols // HEAD_DIM, n_rows, HEAD_DIM), BF16),
        compiler_params=pltpu.CompilerParams(
            dimension_semantics=("arbitrary", "arbitrary"),
            vmem_limit_bytes=VMEM_LIMIT_BYTES),
        name="in_proj",
    )(jnp.asarray(GROUP_EPILOGUE, jnp.int32), x2d, gain, w_bf,
      *([cos_t] * groups_per_tile), *([sin_t] * groups_per_tile))


def _stickbreak_kernel(q_ref, k_ref, v_ref, g_ref, gain_ref, out_ref,
                       acc_ref, carry_ref, tri_ref, bias_ref, z_a, sp_a, z_b, sp_b, *, blk, width):
    seq = q_ref.shape[1]
    n_blk = seq // blk
    n_groups = n_blk // width
    lane_reps = blk // HEAD_DIM
    row = lax.broadcasted_iota(jnp.int32, (blk, blk), 0)
    col = lax.broadcasted_iota(jnp.int32, (blk, blk), 1)
    tri_ref[...] = (row >= col).astype(BF16)
    bias_ref[...] = jnp.where(col < row, 0.0, MASKED_LOGIT)

    def rows_of(i):
        return pl.ds(i * blk if isinstance(i, int) else pl.multiple_of(i * blk, blk), blk)

    def scores(tiles, z_buf, sp_buf):
        for t, (i, j, diagonal, _) in enumerate(tiles):
            z = _nt_dot(q_ref[0, rows_of(i), :], k_ref[0, rows_of(j), :])
            if diagonal:
                z = z + bias_ref[...]
            sp = jnp.maximum(z, 0.0) + jnp.log(1.0 + jnp.exp2(jnp.abs(z) * (-LOG2E)))
            z_buf[t] = z
            sp_buf[t] = sp.astype(BF16)

    def weights(tiles, z_buf, sp_buf, first):
        tri = tri_ref[...]
        csums = [jnp.dot(sp_buf[t], tri, preferred_element_type=F32) for t in range(len(tiles))]
        if not first:
            carries = [carry_ref[rows_of(i), :] for i, _, _, _ in tiles]
            csums_c = [c + jnp.concatenate([carry] * lane_reps, axis=1) for c, carry in zip(csums, carries)]
        else:
            csums_c = csums
        pvs = [jnp.dot(jnp.exp(z_buf[t] - csums_c[t]).astype(BF16), v_ref[0, rows_of(j), :],
                       preferred_element_type=F32) for t, (_, j, _, _) in enumerate(tiles)]
        if not first:
            accs = [acc_ref[rows_of(i), :] for i, _, _, _ in tiles]
        for t, (i, _, _, weight) in enumerate(tiles):
            total = jnp.broadcast_to(csums[t][:, 0:1], (blk, HEAD_DIM))
            pv = pvs[t]
            if weight is not None:
                total = weight * total
                pv = weight * pv
            carry_ref[rows_of(i), :] = total if first else carries[t] + total
            acc_ref[rows_of(i), :] = pv if first else accs[t] + pv

    buffers = ((z_a, sp_a), (z_b, sp_b))

    near = [[(i, i - d, d == 0, None) for i in range(n * width, (n + 1) * width) if i >= d]
            for d in (0, 1) for n in range(n_groups)]
    scores(near[0], *buffers[0])
    for g, tiles in enumerate(near):
        if g + 1 < len(near):
            scores(near[g + 1], *buffers[(g + 1) % 2])
        weights(tiles, *buffers[g % 2], first=(g < n_groups))

    def group_live(n, d):
        live = jnp.bool_(False)
        for i in range(n * width, (n + 1) * width):
            live = live | ((i >= d) & (jnp.min(carry_ref[rows_of(i), :]) < UNDERFLOW_LOG))
        return live

    def any_live(d):
        live = jnp.bool_(False)
        for n in range(n_groups):
            live = live | group_live(n, d)
        return live

    def far_wave(d):
        for n in range(n_groups):
            @pl.when(group_live(n, d))
            def _():
                tiles = [(i, jnp.maximum(i - d, 0), False, (i >= d).astype(F32))
                         for i in range(n * width, (n + 1) * width)]
                scores(tiles, *buffers[0])
                weights(tiles, *buffers[0], first=False)
        return d + 1

    lax.while_loop(lambda d: (d < n_blk) & any_live(d), far_wave, jnp.int32(2))

    def finish(i, _):
        acc = acc_ref[rows_of(i), :]
        ms = jnp.mean(acc * acc, axis=-1, keepdims=True)
        y = (acc * lax.rsqrt(ms + EPS)) * gain_ref[...]
        gate = g_ref[0, rows_of(i), :].astype(F32)
        out_ref[rows_of(i), :] = (_silu(gate) * y).astype(BF16)
        return 0

    lax.fori_loop(0, n_blk, finish, 0, unroll=4)


def _stickbreak(qkvg, gain, batch, seq, n_heads, first_group, *, blk=256, width=4):
    n_rows = batch * seq
    assert (seq // blk) % width == 0
    def slab(group):
        return pl.BlockSpec((1, seq, HEAD_DIM),
                            lambda b, h, group=group: ((first_group + group) * n_heads + h, b, 0))
    kernel = functools.partial(_stickbreak_kernel, blk=blk, width=width)
    stage_buffers = [pltpu.VMEM((width, blk, blk), F32),
                     pltpu.VMEM((width, blk, blk), BF16)]
    return pl.pallas_call(
        kernel,
        grid=(batch, n_heads),
        in_specs=[slab(0), slab(1), slab(2), slab(3),
                  pl.BlockSpec((1, HEAD_DIM), lambda b, h: (0, h))],
        out_specs=pl.BlockSpec((seq, HEAD_DIM), lambda b, h: (b, h)),
        out_shape=jax.ShapeDtypeStruct((n_rows, n_heads * HEAD_DIM), BF16),
        scratch_shapes=[pltpu.VMEM((seq, HEAD_DIM), F32),
                        pltpu.VMEM((seq, HEAD_DIM), F32),
                        pltpu.VMEM((blk, blk), BF16),
                        pltpu.VMEM((blk, blk), F32),
                        ] + stage_buffers + stage_buffers,
        compiler_params=pltpu.CompilerParams(
            dimension_semantics=("arbitrary", "arbitrary"),
            vmem_limit_bytes=VMEM_LIMIT_BYTES),
        name="stickbreak",
    )(qkvg, qkvg, qkvg, qkvg, gain)


def _ret_out_proj_kernel(q_ref, k_ref, v_ref, g_ref, decay_ref, xi_ref, zeta_ref, gc_ref,
                         gn_g_ref, gn_b_ref, sb_ref, x_ref, w_ret_ref, w_sb_ref, fgain_ref, out_ref,
                         r_ref, mix_a, mix_b, *, final_norm, tiles_per_seq):
    s = pl.program_id(0)
    n_tiles = pl.num_programs(0) - 1
    n_heads = q_ref.shape[0]
    n_chunks = q_ref.shape[1] // CHUNK

    @pl.when(s == 0)
    def _():
        r_ref[...] = jnp.zeros_like(r_ref)
        mix_b[...] = jnp.zeros_like(mix_b)

    tile = jnp.minimum(s, n_tiles - 1)
    keep = (tile % tiles_per_seq != 0).astype(F32)

    def retention_chunk(c, mix_out):
        rows = slice(c * CHUNK, (c + 1) * CHUNK)
        heads = range(n_heads)
        qc = [q_ref[h, rows, :] for h in heads]
        kc = [k_ref[h, rows, :] for h in heads]
        vc = [v_ref[h, rows, :] for h in heads]
        rs = [r_ref[h] * keep if c == 0 else r_ref[h] for h in heads]
        scores = [_nt_dot(qc[h], kc[h]) for h in heads]
        kvs = [_tn_dot((kc[h].astype(F32) * zeta_ref[h]).astype(BF16), vc[h]) for h in heads]
        cross = [jnp.dot((qc[h].astype(F32) * xi_ref[h]).astype(BF16), rs[h].astype(BF16),
                         preferred_element_type=F32) for h in heads]
        intra = [jnp.dot((scores[h] * decay_ref[h]).astype(BF16), vc[h], preferred_element_type=F32)
                 for h in heads]
        for h in heads:
            cols = slice(h * HEAD_DIM, (h + 1) * HEAD_DIM)
            r_ref[h] = gc_ref[h] * rs[h] + kvs[h]
            o = intra[h] + cross[h]
            mu = jnp.mean(o, axis=-1, keepdims=True)
            d = o - mu
            var = jnp.mean(d * d, axis=-1, keepdims=True)
            y = (d * lax.rsqrt(var + EPS)) * gn_g_ref[:, cols] + gn_b_ref[:, cols]
            gate = g_ref[h, rows, :].astype(F32)
            mix_out[rows, cols] = (_silu(gate) * y).astype(BF16)

    def project_rows(c, mix_in):
        rows = slice(c * CHUNK, (c + 1) * CHUNK)
        y = (jnp.dot(mix_in[rows, :], w_ret_ref[...], preferred_element_type=F32)
             + jnp.dot(sb_ref[rows, :], w_sb_ref[...], preferred_element_type=F32))
        xn = x_ref[rows, :] + y
        if final_norm:
            ms = jnp.mean(xn * xn, axis=-1, keepdims=True)
            xn = (xn * lax.rsqrt(ms + EPS)) * fgain_ref[...]
        out_ref[rows, :] = xn

    def step(mix_out, mix_in):
        for c in range(n_chunks):
            project_rows(c, mix_in)
            retention_chunk(c, mix_out)

    @pl.when(s % 2 == 0)
    def _():
        step(mix_a, mix_b)

    @pl.when(s % 2 == 1)
    def _():
        step(mix_b, mix_a)


def _ret_out_proj(qkvg, tables, gn_gain, gn_bias, mix_sb, x2d, w_ret, w_sb, fgain, final_norm,
                  seq, n_heads, *, tm=512):
    decay, xi_b, zeta_b, gc_b = tables
    n_rows, d_model = x2d.shape
    ret_w = n_heads * HEAD_DIM
    sb_w = mix_sb.shape[1]
    n_tiles = n_rows // tm
    def slab(group):
        return pl.BlockSpec((n_heads, tm, HEAD_DIM),
                            lambda s, group=group: (group, jnp.minimum(s, n_tiles - 1), 0))
    whole = lambda s: (0, 0, 0)
    prev = lambda s: (jnp.maximum(s - 1, 0), 0)
    kernel = functools.partial(_ret_out_proj_kernel, final_norm=final_norm, tiles_per_seq=seq // tm)
    return pl.pallas_call(
        kernel,
        grid=(n_tiles + 1,),
        in_specs=[
            slab(0), slab(1), slab(2), slab(3),
            pl.BlockSpec((n_heads, CHUNK, CHUNK), whole),
            pl.BlockSpec((n_heads, CHUNK, HEAD_DIM), whole),
            pl.BlockSpec((n_heads, CHUNK, HEAD_DIM), whole),
            pl.BlockSpec((n_heads, 1, HEAD_DIM), whole),
            pl.BlockSpec((1, ret_w), lambda s: (0, 0)),
            pl.BlockSpec((1, ret_w), lambda s: (0, 0)),
            pl.BlockSpec((tm, sb_w), prev),
            pl.BlockSpec((tm, d_model), prev),
            pl.BlockSpec((ret_w, d_model), lambda s: (0, 0)),
            pl.BlockSpec((sb_w, d_model), lambda s: (0, 0)),
            pl.BlockSpec((1, d_model), lambda s: (0, 0)),
        ],
        out_specs=pl.BlockSpec((tm, d_model), prev),
        out_shape=jax.ShapeDtypeStruct((n_rows, d_model), F32),
        scratch_shapes=[pltpu.VMEM((n_heads, HEAD_DIM, HEAD_DIM), F32),
                        pltpu.VMEM((tm, ret_w), BF16),
                        pltpu.VMEM((tm, ret_w), BF16)],
        compiler_params=pltpu.CompilerParams(
            dimension_semantics=("arbitrary",),
            vmem_limit_bytes=VMEM_LIMIT_BYTES),
        name="ret_out_proj",
    )(qkvg, qkvg, qkvg, qkvg, decay, xi_b, zeta_b, gc_b, gn_gain, gn_bias,
      mix_sb, x2d, w_ret, w_sb, fgain)


def _retention_tables(n_heads):
    lg = jnp.log1p(-jnp.exp2(-5.0 - jnp.arange(n_heads, dtype=F32)))
    n = jnp.arange(CHUNK, dtype=F32)
    rel = n[:, None] - n[None, :]
    decay = jnp.where(rel >= 0, jnp.exp(lg[:, None, None] * jnp.maximum(rel, 0.0)), 0.0)
    xi = jnp.exp(lg[:, None] * (n + 1.0))
    zeta = jnp.exp(lg[:, None] * (CHUNK - 1.0 - n))
    gamma_c = jnp.exp(lg * CHUNK)
    xi_b = jnp.broadcast_to(xi[:, :, None], (n_heads, CHUNK, HEAD_DIM))
    zeta_b = jnp.broadcast_to(zeta[:, :, None], (n_heads, CHUNK, HEAD_DIM))
    gc_b = jnp.broadcast_to(gamma_c[:, None, None], (n_heads, 1, HEAD_DIM))
    return decay, xi_b, zeta_b, gc_b


def kernel(x, norm_gain, w_in, ret_gn_gain, ret_gn_bias, sb_norm_gain, w_out, final_norm_gain):
    batch, seq, d_model = x.shape
    depth = norm_gain.shape[0]
    ret_width = ret_gn_gain.shape[1]
    sb_width = sb_norm_gain.shape[1]
    ret_heads = ret_width // HEAD_DIM
    sb_heads = sb_width // HEAD_DIM
    assert ret_heads == sb_heads and w_in.shape[2] == N_GROUPS * ret_width

    cos_t, sin_t = _epilogue_tables(seq)
    tables = _retention_tables(ret_heads)
    fgain = final_norm_gain.reshape(1, d_model)
    x2d = x.reshape(batch * seq, d_model)
    for l in range(depth):
        qkvg = _in_proj(x2d, norm_gain[l].reshape(1, d_model), w_in[l].astype(BF16), cos_t, sin_t, seq)
        mix_sb = _stickbreak(qkvg, sb_norm_gain[l].reshape(1, sb_width), batch, seq, sb_heads, 4)
        w_out_bf = w_out[l].astype(BF16)
        x2d = _ret_out_proj(qkvg, tables, ret_gn_gain[l].reshape(1, ret_width),
                            ret_gn_bias[l].reshape(1, ret_width), mix_sb, x2d,
                            w_out_bf[:ret_width], w_out_bf[ret_width:], fgain,
                            (l == depth - 1), seq, ret_heads)
    return x2d.reshape(batch, seq, d_model)
```

```python
import functools

import jax
import jax.numpy as jnp
from jax import lax
from jax.experimental import pallas as pl
from jax.experimental.pallas import tpu as pltpu

HEAD_DIM = 128
CHUNK = 128
ROPE_THETA = 10000.0
EPS = 1e-6
LOG2E = 1.4426950408889634
MASKED_LOGIT = -1e30
UNDERFLOW_LOG = 105.0
N_GROUPS = 8
IN_PROJ_ROW_SLAB = 256
RETENTION_HEADS_AT_ONCE = 4

F32 = jnp.float32
BF16 = jnp.bfloat16

VMEM_LIMIT_BYTES = 60 * 1024 * 1024


def _nt_dot(a, b):
    return lax.dot_general(a, b, (((1,), (1,)), ((), ())), preferred_element_type=F32)


def _tn_dot(a, b):
    return lax.dot_general(a, b, (((0,), (0,)), ((), ())), preferred_element_type=F32)


def _silu(g):
    return (0.5 * g) * (1.0 + jnp.tanh(0.5 * g))


EPILOGUE_ROTARY, EPILOGUE_ROTARY_SCALED, EPILOGUE_SCALED, EPILOGUE_IDENTITY = range(4)
GROUP_EPILOGUE = (EPILOGUE_ROTARY, EPILOGUE_ROTARY_SCALED, EPILOGUE_IDENTITY, EPILOGUE_IDENTITY,
                  EPILOGUE_SCALED, EPILOGUE_IDENTITY, EPILOGUE_IDENTITY, EPILOGUE_IDENTITY)


def _in_proj_kernel(kind_tab, x_ref, gain_ref, w_ref, *refs):
    n_tables = (len(refs) - 2) // 2
    cos_refs, sin_refs = refs[:n_tables], refs[n_tables:2 * n_tables]
    out_ref, h_ref = refs[2 * n_tables:]
    heads_per_group = out_ref.shape[0] // n_tables

    @pl.when(pl.program_id(1) == 0)
    def _():
        xf = x_ref[...]
        ms = jnp.mean(xf * xf, axis=-1, keepdims=True)
        h_ref[...] = ((xf * lax.rsqrt(ms + EPS)) * gain_ref[...]).astype(BF16)

    def project(with_sin):
        w = w_ref[...]
        for r in range(0, h_ref.shape[0], IN_PROJ_ROW_SLAB):
            rows = slice(r, r + IN_PROJ_ROW_SLAB)
            acc = jnp.dot(h_ref[rows, :], w, preferred_element_type=F32)
            for c in range(out_ref.shape[0]):
                blk = acc[:, c * HEAD_DIM:(c + 1) * HEAD_DIM]
                y = blk * cos_refs[c // heads_per_group][0, rows, :]
                if with_sin:
                    y = y + pltpu.roll(blk, HEAD_DIM // 2, 1) * sin_refs[c // heads_per_group][0, rows, :]
                out_ref[c, rows, :] = y.astype(BF16)

    j = pl.program_id(1)
    has_sin = kind_tab[j * n_tables] <= EPILOGUE_ROTARY_SCALED
    for g in range(1, n_tables):
        has_sin = has_sin | (kind_tab[j * n_tables + g] <= EPILOGUE_ROTARY_SCALED)
    pl.when(has_sin)(functools.partial(project, True))
    pl.when(jnp.logical_not(has_sin))(functools.partial(project, False))


def _epilogue_tables(seq):
    half = HEAD_DIM // 2
    inv = ROPE_THETA ** (-jnp.arange(half, dtype=F32) / half)
    ang = jnp.arange(seq, dtype=F32)[:, None] * inv[None, :]
    cos = jnp.concatenate([jnp.cos(ang)] * 2, axis=-1)
    sin = jnp.concatenate([-jnp.sin(ang), jnp.sin(ang)], axis=-1)
    ones = jnp.ones_like(cos)
    zeros = jnp.zeros_like(sin)
    scale = HEAD_DIM ** -0.5
    return (jnp.stack([cos, cos * scale, ones * scale, ones]),
            jnp.stack([sin, sin * scale, zeros, zeros]))


def _in_proj(x2d, gain, w_bf, cos_t, sin_t, seq, *, tm=1024, groups_per_tile=2):
    n_rows, d_model = x2d.shape
    in_cols = w_bf.shape[1]
    tn = groups_per_tile * (in_cols // N_GROUPS)
    heads_per_tile = tn // HEAD_DIM
    seq_tiles = seq // tm
    assert tm % IN_PROJ_ROW_SLAB == 0 and N_GROUPS % groups_per_tile == 0
    def table(g):
        return pl.BlockSpec((1, tm, HEAD_DIM),
                            lambda i, j, kind: (kind[j * groups_per_tile + g], i % seq_tiles, 0))
    tables = [table(g) for g in range(groups_per_tile)]
    grid_spec = pltpu.PrefetchScalarGridSpec(
        num_scalar_prefetch=1,
        grid=(n_rows // tm, N_GROUPS // groups_per_tile),
        in_specs=[
            pl.BlockSpec((tm, d_model), lambda i, j, kind: (i, 0)),
            pl.BlockSpec((1, d_model), lambda i, j, kind: (0, 0)),
            pl.BlockSpec((d_model, tn), lambda i, j, kind: (0, j)),
        ] + tables + tables,
        out_specs=pl.BlockSpec((heads_per_tile, tm, HEAD_DIM), lambda i, j, kind: (j, i, 0)),
        scratch_shapes=[pltpu.VMEM((tm, d_model), BF16)],
    )
    return pl.pallas_call(
        _in_proj_kernel,
        grid_spec=grid_spec,
        out_shape=jax.ShapeDtypeStruct((in_cols // HEAD_DIM, n_rows, HEAD_DIM), BF16),
        compiler_params=pltpu.CompilerParams(
            dimension_semantics=("arbitrary", "arbitrary"),
            vmem_limit_bytes=VMEM_LIMIT_BYTES),
        name="in_proj",
    )(jnp.asarray(GROUP_EPILOGUE, jnp.int32), x2d, gain, w_bf,
      *([cos_t] * groups_per_tile), *([sin_t] * groups_per_tile))


def _stickbreak_kernel(q_ref, k_ref, v_ref, g_ref, gain_ref, out_ref,
                       acc_ref, carry_ref, tri_ref, bias_ref, z_a, sp_a, z_b, sp_b, *, blk, width):
    seq = q_ref.shape[1]
    n_blk = seq // blk
    n_groups = n_blk // width
    row = lax.broadcasted_iota(jnp.int32, (blk, blk), 0)
    col = lax.broadcasted_iota(jnp.int32, (blk, blk), 1)
    tri_ref[...] = (row >= col).astype(BF16)
    bias_ref[...] = jnp.where(col < row, 0.0, MASKED_LOGIT)

    def rows(start, size):
        return pl.ds(start if isinstance(start, int) else pl.multiple_of(start, blk), size)

    def scores(tiles, z_buf, sp_buf):
        for slot, r0, q0, m, k0, n, diagonal, _ in tiles:
            z = _nt_dot(q_ref[0, rows(q0, m), :], k_ref[0, rows(k0, n), :])
            if diagonal:
                z = z + bias_ref[r0:r0 + m, 0:n]
            sp = jnp.maximum(z, 0.0) + jnp.log(1.0 + jnp.exp2(jnp.abs(z) * (-LOG2E)))
            z_buf[slot, r0:r0 + m, 0:n] = z
            sp_buf[slot, r0:r0 + m, 0:n] = sp.astype(BF16)

    def weights(tiles, z_buf, sp_buf, first):
        csums = [jnp.dot(sp_buf[slot, r0:r0 + m, 0:n], tri_ref[0:n, 0:n], preferred_element_type=F32)
                 for slot, r0, _, m, _, n, _, _ in tiles]
        if not first:
            carries = [carry_ref[rows(q0, m), :] for _, _, q0, m, _, _, _, _ in tiles]
            csums_c = [c + jnp.concatenate([carry] * (c.shape[1] // HEAD_DIM), axis=1)
                       for c, carry in zip(csums, carries)]
        else:
            csums_c = csums
        pvs = [jnp.dot(jnp.exp(z_buf[slot, r0:r0 + m, 0:n] - c).astype(BF16), v_ref[0, rows(k0, n), :],
                       preferred_element_type=F32)
               for (slot, r0, _, m, k0, n, _, _), c in zip(tiles, csums_c)]
        if not first:
            accs = [acc_ref[rows(q0, m), :] for _, _, q0, m, _, _, _, _ in tiles]
        for t, (_, _, q0, m, _, _, _, weight) in enumerate(tiles):
            total = jnp.broadcast_to(csums[t][:, 0:1], (m, HEAD_DIM))
            pv = pvs[t]
            if weight is not None:
                total = weight * total
                pv = weight * pv
            carry_ref[rows(q0, m), :] = total if first else carries[t] + total
            acc_ref[rows(q0, m), :] = pv if first else accs[t] + pv

    buffers = ((z_a, sp_a), (z_b, sp_b))
    half = blk // 2

    def diagonal_tiles(slot, i):
        return [(slot, 0, i * blk, half, i * blk, half, True, None),
                (slot, half, i * blk + half, half, i * blk, blk, True, None)]

    near = ([[tile for t in range(width) for tile in diagonal_tiles(t, n * width + t)]
             for n in range(n_groups)]
            + [[(i % width, 0, i * blk, blk, (i - 1) * blk, blk, False, None)
                for i in range(n * width, (n + 1) * width) if i >= 1] for n in range(n_groups)])
    scores(near[0], *buffers[0])
    for g, tiles in enumerate(near):
        if g + 1 < len(near):
            scores(near[g + 1], *buffers[(g + 1) % 2])
        weights(tiles, *buffers[g % 2], first=(g < n_groups))

    def group_live(n, d):
        live = jnp.bool_(False)
        for i in range(n * width, (n + 1) * width):
            live = live | ((i >= d) & (jnp.min(carry_ref[rows(i * blk, blk), :]) < UNDERFLOW_LOG))
        return live

    def any_live(d):
        live = jnp.bool_(False)
        for n in range(n_groups):
            live = live | group_live(n, d)
        return live

    def far_wave(d):
        for n in range(n_groups):
            @pl.when(group_live(n, d))
            def _():
                tiles = [(i % width, 0, i * blk, blk, jnp.maximum(i - d, 0) * blk, blk, False,
                          (i >= d).astype(F32)) for i in range(n * width, (n + 1) * width)]
                scores(tiles, *buffers[0])
                weights(tiles, *buffers[0], first=False)
        return d + 1

    lax.while_loop(lambda d: (d < n_blk) & any_live(d), far_wave, jnp.int32(2))

    def finish(i, _):
        block = rows(i * blk, blk)
        acc = acc_ref[block, :]
        ms = jnp.mean(acc * acc, axis=-1, keepdims=True)
        y = (acc * lax.rsqrt(ms + EPS)) * gain_ref[...]
        gate = g_ref[0, block, :].astype(F32)
        out_ref[block, :] = (_silu(gate) * y).astype(BF16)
        return 0

    lax.fori_loop(0, n_blk, finish, 0, unroll=4)


def _stickbreak(qkvg, gain, batch, seq, n_heads, first_group, *, blk=256, width=4):
    n_rows = batch * seq
    assert (seq // blk) % width == 0
    def slab(group):
        return pl.BlockSpec((1, seq, HEAD_DIM),
                            lambda b, h, group=group: ((first_group + group) * n_heads + h, b, 0))
    kernel = functools.partial(_stickbreak_kernel, blk=blk, width=width)
    stage_buffers = [pltpu.VMEM((width, blk, blk), F32),
                     pltpu.VMEM((width, blk, blk), BF16)]
    return pl.pallas_call(
        kernel,
        grid=(batch, n_heads),
        in_specs=[slab(0), slab(1), slab(2), slab(3),
                  pl.BlockSpec((1, HEAD_DIM), lambda b, h: (0, h))],
        out_specs=pl.BlockSpec((seq, HEAD_DIM), lambda b, h: (b, h)),
        out_shape=jax.ShapeDtypeStruct((n_rows, n_heads * HEAD_DIM), BF16),
        scratch_shapes=[pltpu.VMEM((seq, HEAD_DIM), F32),
                        pltpu.VMEM((seq, HEAD_DIM), F32),
                        pltpu.VMEM((blk, blk), BF16),
                        pltpu.VMEM((blk, blk), F32),
                        ] + stage_buffers + stage_buffers,
        compiler_params=pltpu.CompilerParams(
            dimension_semantics=("arbitrary", "arbitrary"),
            vmem_limit_bytes=VMEM_LIMIT_BYTES),
        name="stickbreak",
    )(qkvg, qkvg, qkvg, qkvg, gain)


def _ret_out_proj_kernel(q_ref, k_ref, v_ref, g_ref, decay_ref, xi_ref, zeta_ref, gc_ref,
                         gn_g_ref, gn_b_ref, sb_ref, x_ref, w_ret_ref, w_sb_ref, fgain_ref, out_ref,
                         r_ref, mix_a, mix_b, *, final_norm, tiles_per_seq):
    s = pl.program_id(0)
    n_tiles = pl.num_programs(0) - 1
    n_heads = q_ref.shape[0]
    n_chunks = q_ref.shape[1] // CHUNK

    @pl.when(s == 0)
    def _():
        r_ref[...] = jnp.zeros_like(r_ref)
        mix_b[...] = jnp.zeros_like(mix_b)

    tile = jnp.minimum(s, n_tiles - 1)
    keep = (tile % tiles_per_seq != 0).astype(F32)

    def retention_chunk(c, mix_out):
        for h0 in range(0, n_heads, RETENTION_HEADS_AT_ONCE):
            retention_heads(c, range(h0, h0 + RETENTION_HEADS_AT_ONCE), mix_out)

    def retention_heads(c, heads, mix_out):
        rows = slice(c * CHUNK, (c + 1) * CHUNK)
        qc = {h: q_ref[h, rows, :] for h in heads}
        kc = {h: k_ref[h, rows, :] for h in heads}
        vc = {h: v_ref[h, rows, :] for h in heads}
        rs = {h: r_ref[h] * keep if c == 0 else r_ref[h] for h in heads}
        scores = {h: _nt_dot(qc[h], kc[h]) for h in heads}
        kvs = {h: _tn_dot((kc[h].astype(F32) * zeta_ref[h]).astype(BF16), vc[h]) for h in heads}
        cross = {h: jnp.dot((qc[h].astype(F32) * xi_ref[h]).astype(BF16), rs[h].astype(BF16),
                            preferred_element_type=F32) for h in heads}
        intra = {h: jnp.dot((scores[h] * decay_ref[h]).astype(BF16), vc[h], preferred_element_type=F32)
                 for h in heads}
        for h in heads:
            cols = slice(h * HEAD_DIM, (h + 1) * HEAD_DIM)
            r_ref[h] = gc_ref[h] * rs[h] + kvs[h]
            o = intra[h] + cross[h]
            mu = jnp.mean(o, axis=-1, keepdims=True)
            d = o - mu
            var = jnp.mean(d * d, axis=-1, keepdims=True)
            y = (d * lax.rsqrt(var + EPS)) * gn_g_ref[:, cols] + gn_b_ref[:, cols]
            gate = g_ref[h, rows, :].astype(F32)
            mix_out[rows, cols] = (_silu(gate) * y).astype(BF16)

    def project_rows(c, mix_in):
        rows = slice(c * CHUNK, (c + 1) * CHUNK)
        y = (jnp.dot(mix_in[rows, :], w_ret_ref[...], preferred_element_type=F32)
             + jnp.dot(sb_ref[rows, :], w_sb_ref[...], preferred_element_type=F32))
        xn = x_ref[rows, :] + y
        if final_norm:
            ms = jnp.mean(xn * xn, axis=-1, keepdims=True)
            xn = (xn * lax.rsqrt(ms + EPS)) * fgain_ref[...]
        out_ref[rows, :] = xn

    def step(mix_out, mix_in):
        for c in range(n_chunks):
            project_rows(c, mix_in)
            retention_chunk(c, mix_out)

    @pl.when(s % 2 == 0)
    def _():
        step(mix_a, mix_b)

    @pl.when(s % 2 == 1)
    def _():
        step(mix_b, mix_a)


def _ret_out_proj(qkvg, tables, gn_gain, gn_bias, mix_sb, x2d, w_ret, w_sb, fgain, final_norm,
                  seq, n_heads, *, tm=512):
    decay, xi_b, zeta_b, gc_b = tables
    n_rows, d_model = x2d.shape
    ret_w = n_heads * HEAD_DIM
    sb_w = mix_sb.shape[1]
    n_tiles = n_rows // tm
    def slab(group):
        return pl.BlockSpec((n_heads, tm, HEAD_DIM),
                            lambda s, group=group: (group, jnp.minimum(s, n_tiles - 1), 0))
    whole = lambda s: (0, 0, 0)
    prev = lambda s: (jnp.maximum(s - 1, 0), 0)
    kernel = functools.partial(_ret_out_proj_kernel, final_norm=final_norm, tiles_per_seq=seq // tm)
    return pl.pallas_call(
        kernel,
        grid=(n_tiles + 1,),
        in_specs=[
            slab(0), slab(1), slab(2), slab(3),
            pl.BlockSpec((n_heads, CHUNK, CHUNK), whole),
            pl.BlockSpec((n_heads, CHUNK, HEAD_DIM), whole),
            pl.BlockSpec((n_heads, CHUNK, HEAD_DIM), whole),
            pl.BlockSpec((n_heads, 1, HEAD_DIM), whole),
            pl.BlockSpec((1, ret_w), lambda s: (0, 0)),
            pl.BlockSpec((1, ret_w), lambda s: (0, 0)),
            pl.BlockSpec((tm, sb_w), prev),
            pl.BlockSpec((tm, d_model), prev),
            pl.BlockSpec((ret_w, d_model), lambda s: (0, 0)),
            pl.BlockSpec((sb_w, d_model), lambda s: (0, 0)),
            pl.BlockSpec((1, d_model), lambda s: (0, 0)),
        ],
        out_specs=pl.BlockSpec((tm, d_model), prev),
        out_shape=jax.ShapeDtypeStruct((n_rows, d_model), F32),
        scratch_shapes=[pltpu.VMEM((n_heads, HEAD_DIM, HEAD_DIM), F32),
                        pltpu.VMEM((tm, ret_w), BF16),
                        pltpu.VMEM((tm, ret_w), BF16)],
        compiler_params=pltpu.CompilerParams(
            dimension_semantics=("arbitrary",),
            vmem_limit_bytes=VMEM_LIMIT_BYTES),
        name="ret_out_proj",
    )(qkvg, qkvg, qkvg, qkvg, decay, xi_b, zeta_b, gc_b, gn_gain, gn_bias,
      mix_sb, x2d, w_ret, w_sb, fgain)


def _retention_tables(n_heads):
    lg = jnp.log1p(-jnp.exp2(-5.0 - jnp.arange(n_heads, dtype=F32)))
    n = jnp.arange(CHUNK, dtype=F32)
    rel = n[:, None] - n[None, :]
    decay = jnp.where(rel >= 0, jnp.exp(lg[:, None, None] * jnp.maximum(rel, 0.0)), 0.0)
    xi = jnp.exp(lg[:, None] * (n + 1.0))
    zeta = jnp.exp(lg[:, None] * (CHUNK - 1.0 - n))
    gamma_c = jnp.exp(lg * CHUNK)
    xi_b = jnp.broadcast_to(xi[:, :, None], (n_heads, CHUNK, HEAD_DIM))
    zeta_b = jnp.broadcast_to(zeta[:, :, None], (n_heads, CHUNK, HEAD_DIM))
    gc_b = jnp.broadcast_to(gamma_c[:, None, None], (n_heads, 1, HEAD_DIM))
    return decay, xi_b, zeta_b, gc_b


def kernel(x, norm_gain, w_in, ret_gn_gain, ret_gn_bias, sb_norm_gain, w_out, final_norm_gain):
    batch, seq, d_model = x.shape
    depth = norm_gain.shape[0]
    ret_width = ret_gn_gain.shape[1]
    sb_width = sb_norm_gain.shape[1]
    ret_heads = ret_width // HEAD_DIM
    sb_heads = sb_width // HEAD_DIM
    assert ret_heads == sb_heads and w_in.shape[2] == N_GROUPS * ret_width

    cos_t, sin_t = _epilogue_tables(seq)
    tables = _retention_tables(ret_heads)
    fgain = final_norm_gain.reshape(1, d_model)
    x2d = x.reshape(batch * seq, d_model)
    for l in range(depth):
        qkvg = _in_proj(x2d, norm_gain[l].reshape(1, d_model), w_in[l].astype(BF16), cos_t, sin_t, seq)
        mix_sb = _stickbreak(qkvg, sb_norm_gain[l].reshape(1, sb_width), batch, seq, sb_heads, 4)
        w_out_bf = w_out[l].astype(BF16)
        x2d = _ret_out_proj(qkvg, tables, ret_gn_gain[l].reshape(1, ret_width),
                            ret_gn_bias[l].reshape(1, ret_width), mix_sb, x2d,
                            w_out_bf[:ret_width], w_out_bf[ret_width:], fgain,
                            (l == depth - 1), seq, ret_heads)
    return x2d.reshape(batch, seq, d_model)
```

```python
import functools

import jax
import jax.numpy as jnp
from jax import lax
from jax.experimental import pallas as pl
from jax.experimental.pallas import tpu as pltpu

HEAD_DIM = 128
CHUNK = 128
ROPE_THETA = 10000.0
EPS = 1e-6
LOG2E = 1.4426950408889634
MASKED_LOGIT = -1e30
UNDERFLOW_LOG = 105.0
N_GROUPS = 8
IN_PROJ_ROW_SLAB = 256
RETENTION_HEADS_AT_ONCE = 4
OUT_PROJ_SLAB_CHUNKS = 2

F32 = jnp.float32
BF16 = jnp.bfloat16

VMEM_LIMIT_BYTES = 60 * 1024 * 1024


def _nt_dot(a, b):
    return lax.dot_general(a, b, (((1,), (1,)), ((), ())), preferred_element_type=F32)


def _tn_dot(a, b):
    return lax.dot_general(a, b, (((0,), (0,)), ((), ())), preferred_element_type=F32)


def _silu(g):
    return (0.5 * g) * (1.0 + jnp.tanh(0.5 * g))


EPILOGUE_ROTARY, EPILOGUE_ROTARY_SCALED, EPILOGUE_SCALED, EPILOGUE_IDENTITY = range(4)
GROUP_EPILOGUE = (EPILOGUE_ROTARY, EPILOGUE_ROTARY_SCALED, EPILOGUE_IDENTITY, EPILOGUE_IDENTITY,
                  EPILOGUE_SCALED, EPILOGUE_IDENTITY, EPILOGUE_IDENTITY, EPILOGUE_IDENTITY)


def _in_proj_kernel(kind_tab, x_ref, gain_ref, w_ref, *refs):
    n_tables = (len(refs) - 2) // 2
    cos_refs, sin_refs = refs[:n_tables], refs[n_tables:2 * n_tables]
    out_ref, h_ref = refs[2 * n_tables:]
    heads_per_group = out_ref.shape[0] // n_tables

    @pl.when(pl.program_id(1) == 0)
    def _():
        xf = x_ref[...]
        ms = jnp.mean(xf * xf, axis=-1, keepdims=True)
        h_ref[...] = ((xf * lax.rsqrt(ms + EPS)) * gain_ref[...]).astype(BF16)

    def project(with_sin):
        w = w_ref[...]
        for r in range(0, h_ref.shape[0], IN_PROJ_ROW_SLAB):
            rows = slice(r, r + IN_PROJ_ROW_SLAB)
            acc = jnp.dot(h_ref[rows, :], w, preferred_element_type=F32)
            for c in range(out_ref.shape[0]):
                blk = acc[:, c * HEAD_DIM:(c + 1) * HEAD_DIM]
                y = blk * cos_refs[c // heads_per_group][0, rows, :]
                if with_sin:
                    y = y + pltpu.roll(blk, HEAD_DIM // 2, 1) * sin_refs[c // heads_per_group][0, rows, :]
                out_ref[c, rows, :] = y.astype(BF16)

    j = pl.program_id(1)
    has_sin = kind_tab[j * n_tables] <= EPILOGUE_ROTARY_SCALED
    for g in range(1, n_tables):
        has_sin = has_sin | (kind_tab[j * n_tables + g] <= EPILOGUE_ROTARY_SCALED)
    pl.when(has_sin)(functools.partial(project, True))
    pl.when(jnp.logical_not(has_sin))(functools.partial(project, False))


def _epilogue_tables(seq):
    half = HEAD_DIM // 2
    inv = ROPE_THETA ** (-jnp.arange(half, dtype=F32) / half)
    ang = jnp.arange(seq, dtype=F32)[:, None] * inv[None, :]
    cos = jnp.concatenate([jnp.cos(ang)] * 2, axis=-1)
    sin = jnp.concatenate([-jnp.sin(ang), jnp.sin(ang)], axis=-1)
    ones = jnp.ones_like(cos)
    zeros = jnp.zeros_like(sin)
    scale = HEAD_DIM ** -0.5
    return (jnp.stack([cos, cos * scale, ones * scale, ones]),
            jnp.stack([sin, sin * scale, zeros, zeros]))


def _in_proj(x2d, gain, w_bf, cos_t, sin_t, seq, *, tm=1024, groups_per_tile=2):
    n_rows, d_model = x2d.shape
    in_cols = w_bf.shape[1]
    tn = groups_per_tile * (in_cols // N_GROUPS)
    heads_per_tile = tn // HEAD_DIM
    seq_tiles = seq // tm
    assert tm % IN_PROJ_ROW_SLAB == 0 and N_GROUPS % groups_per_tile == 0
    def table(g):
        return pl.BlockSpec((1, tm, HEAD_DIM),
                            lambda i, j, kind: (kind[j * groups_per_tile + g], i % seq_tiles, 0))
    tables = [table(g) for g in range(groups_per_tile)]
    grid_spec = pltpu.PrefetchScalarGridSpec(
        num_scalar_prefetch=1,
        grid=(n_rows // tm, N_GROUPS // groups_per_tile),
        in_specs=[
            pl.BlockSpec((tm, d_model), lambda i, j, kind: (i, 0)),
            pl.BlockSpec((1, d_model), lambda i, j, kind: (0, 0)),
            pl.BlockSpec((d_model, tn), lambda i, j, kind: (0, j)),
        ] + tables + tables,
        out_specs=pl.BlockSpec((heads_per_tile, tm, HEAD_DIM), lambda i, j, kind: (j, i, 0)),
        scratch_shapes=[pltpu.VMEM((tm, d_model), BF16)],
    )
    return pl.pallas_call(
        _in_proj_kernel,
        grid_spec=grid_spec,
        out_shape=jax.ShapeDtypeStruct((in_cols // HEAD_DIM, n_rows, HEAD_DIM), BF16),
        compiler_params=pltpu.CompilerParams(
            dimension_semantics=("arbitrary", "arbitrary"),
            vmem_limit_bytes=VMEM_LIMIT_BYTES),
        name="in_proj",
    )(jnp.asarray(GROUP_EPILOGUE, jnp.int32), x2d, gain, w_bf,
      *([cos_t] * groups_per_tile), *([sin_t] * groups_per_tile))


def _stickbreak_kernel(q_ref, k_ref, v_ref, g_ref, gain_ref, out_ref,
                       acc_ref, carry_ref, tri_ref, bias_ref, z_a, sp_a, z_b, sp_b, *, blk, width):
    seq = q_ref.shape[1]
    n_blk = seq // blk
    n_groups = n_blk // width
    row = lax.broadcasted_iota(jnp.int32, (blk, blk), 0)
    col = lax.broadcasted_iota(jnp.int32, (blk, blk), 1)
    tri_ref[...] = (row >= col).astype(BF16)
    bias_ref[...] = jnp.where(col < row, 0.0, MASKED_LOGIT)

    def rows(start, size):
        return pl.ds(start if isinstance(start, int) else pl.multiple_of(start, blk), size)

    def scores(tiles, z_buf, sp_buf):
        for slot, r0, q0, m, k0, n, diagonal, _ in tiles:
            z = _nt_dot(q_ref[0, rows(q0, m), :], k_ref[0, rows(k0, n), :])
            if diagonal:
                z = z + bias_ref[r0:r0 + m, 0:n]
            sp = jnp.maximum(z, 0.0) + jnp.log(1.0 + jnp.exp2(jnp.abs(z) * (-LOG2E)))
            z_buf[slot, r0:r0 + m, 0:n] = z
            sp_buf[slot, r0:r0 + m, 0:n] = sp.astype(BF16)

    def weights(tiles, z_buf, sp_buf, first):
        csums = [jnp.dot(sp_buf[slot, r0:r0 + m, 0:n], tri_ref[0:n, 0:n], preferred_element_type=F32)
                 for slot, r0, _, m, _, n, _, _ in tiles]
        if not first:
            carries = [carry_ref[rows(q0, m), :] for _, _, q0, m, _, _, _, _ in tiles]
            csums_c = [c + jnp.concatenate([carry] * (c.shape[1] // HEAD_DIM), axis=1)
                       for c, carry in zip(csums, carries)]
        else:
            csums_c = csums
        pvs = [jnp.dot(jnp.exp(z_buf[slot, r0:r0 + m, 0:n] - c).astype(BF16), v_ref[0, rows(k0, n), :],
                       preferred_element_type=F32)
               for (slot, r0, _, m, k0, n, _, _), c in zip(tiles, csums_c)]
        if not first:
            accs = [acc_ref[rows(q0, m), :] for _, _, q0, m, _, _, _, _ in tiles]
        for t, (_, _, q0, m, _, _, _, weight) in enumerate(tiles):
            total = jnp.broadcast_to(csums[t][:, 0:1], (m, HEAD_DIM))
            pv = pvs[t]
            if weight is not None:
                total = weight * total
                pv = weight * pv
            carry_ref[rows(q0, m), :] = total if first else carries[t] + total
            acc_ref[rows(q0, m), :] = pv if first else accs[t] + pv

    buffers = ((z_a, sp_a), (z_b, sp_b))
    half = blk // 2

    def diagonal_tiles(slot, i):
        return [(slot, 0, i * blk, half, i * blk, half, True, None),
                (slot, half, i * blk + half, half, i * blk, blk, True, None)]

    near = ([[tile for t in range(width) for tile in diagonal_tiles(t, n * width + t)]
             for n in range(n_groups)]
            + [[(i % width, 0, i * blk, blk, (i - 1) * blk, blk, False, None)
                for i in range(n * width, (n + 1) * width) if i >= 1] for n in range(n_groups)])
    scores(near[0], *buffers[0])
    for g, tiles in enumerate(near):
        if g + 1 < len(near):
            scores(near[g + 1], *buffers[(g + 1) % 2])
        weights(tiles, *buffers[g % 2], first=(g < n_groups))

    def group_live(n, d):
        live = jnp.bool_(False)
        for i in range(n * width, (n + 1) * width):
            live = live | ((i >= d) & (jnp.min(carry_ref[rows(i * blk, blk), :]) < UNDERFLOW_LOG))
        return live

    def any_live(d):
        live = jnp.bool_(False)
        for n in range(n_groups):
            live = live | group_live(n, d)
        return live

    def far_wave(d):
        for n in range(n_groups):
            @pl.when(group_live(n, d))
            def _():
                tiles = [(i % width, 0, i * blk, blk, jnp.maximum(i - d, 0) * blk, blk, False,
                          (i >= d).astype(F32)) for i in range(n * width, (n + 1) * width)]
                scores(tiles, *buffers[0])
                weights(tiles, *buffers[0], first=False)
        return d + 1

    lax.while_loop(lambda d: (d < n_blk) & any_live(d), far_wave, jnp.int32(2))

    def finish(i, _):
        block = rows(i * blk, blk)
        acc = acc_ref[block, :]
        ms = jnp.mean(acc * acc, axis=-1, keepdims=True)
        y = (acc * lax.rsqrt(ms + EPS)) * gain_ref[...]
        gate = g_ref[0, block, :].astype(F32)
        out_ref[block, :] = (_silu(gate) * y).astype(BF16)
        return 0

    lax.fori_loop(0, n_blk, finish, 0, unroll=True)


def _stickbreak(qkvg, gain, batch, seq, n_heads, first_group, *, blk=256, width=4):
    n_rows = batch * seq
    assert (seq // blk) % width == 0
    def slab(group):
        return pl.BlockSpec((1, seq, HEAD_DIM),
                            lambda b, h, group=group: ((first_group + group) * n_heads + h, b, 0))
    kernel = functools.partial(_stickbreak_kernel, blk=blk, width=width)
    stage_buffers = [pltpu.VMEM((width, blk, blk), F32),
                     pltpu.VMEM((width, blk, blk), BF16)]
    return pl.pallas_call(
        kernel,
        grid=(batch, n_heads),
        in_specs=[slab(0), slab(1), slab(2), slab(3),
                  pl.BlockSpec((1, HEAD_DIM), lambda b, h: (0, h))],
        out_specs=pl.BlockSpec((seq, HEAD_DIM), lambda b, h: (b, h)),
        out_shape=jax.ShapeDtypeStruct((n_rows, n_heads * HEAD_DIM), BF16),
        scratch_shapes=[pltpu.VMEM((seq, HEAD_DIM), F32),
                        pltpu.VMEM((seq, HEAD_DIM), F32),
                        pltpu.VMEM((blk, blk), BF16),
                        pltpu.VMEM((blk, blk), F32),
                        ] + stage_buffers + stage_buffers,
        compiler_params=pltpu.CompilerParams(
            dimension_semantics=("arbitrary", "arbitrary"),
            vmem_limit_bytes=VMEM_LIMIT_BYTES),
        name="stickbreak",
    )(qkvg, qkvg, qkvg, qkvg, gain)


def _ret_out_proj_kernel(q_ref, k_ref, v_ref, g_ref, decay_ref, xi_ref, zeta_ref, gc_ref,
                         gn_g_ref, gn_b_ref, sb_ref, x_ref, w_ret_ref, w_sb_ref, fgain_ref, out_ref,
                         r_ref, mix_a, mix_b, *, final_norm, tiles_per_seq):
    s = pl.program_id(0)
    n_tiles = pl.num_programs(0) - 1
    n_heads = q_ref.shape[0]
    n_chunks = q_ref.shape[1] // CHUNK

    @pl.when(s == 0)
    def _():
        r_ref[...] = jnp.zeros_like(r_ref)
        mix_b[...] = jnp.zeros_like(mix_b)

    tile = jnp.minimum(s, n_tiles - 1)
    keep = (tile % tiles_per_seq != 0).astype(F32)

    def retention_chunk(c, mix_out):
        for h0 in range(0, n_heads, RETENTION_HEADS_AT_ONCE):
            retention_heads(c, range(h0, h0 + RETENTION_HEADS_AT_ONCE), mix_out)

    def retention_heads(c, heads, mix_out):
        rows = slice(c * CHUNK, (c + 1) * CHUNK)
        qc = {h: q_ref[h, rows, :] for h in heads}
        kc = {h: k_ref[h, rows, :] for h in heads}
        vc = {h: v_ref[h, rows, :] for h in heads}
        rs = {h: r_ref[h] * keep if c == 0 else r_ref[h] for h in heads}
        scores = {h: _nt_dot(qc[h], kc[h]) for h in heads}
        kvs = {h: _tn_dot((kc[h].astype(F32) * zeta_ref[h]).astype(BF16), vc[h]) for h in heads}
        cross = {h: jnp.dot((qc[h].astype(F32) * xi_ref[h]).astype(BF16), rs[h].astype(BF16),
                            preferred_element_type=F32) for h in heads}
        intra = {h: jnp.dot((scores[h] * decay_ref[h]).astype(BF16), vc[h], preferred_element_type=F32)
                 for h in heads}
        for h in heads:
            cols = slice(h * HEAD_DIM, (h + 1) * HEAD_DIM)
            r_ref[h] = gc_ref[h] * rs[h] + kvs[h]
            o = intra[h] + cross[h]
            mu = jnp.mean(o, axis=-1, keepdims=True)
            d = o - mu
            var = jnp.mean(d * d, axis=-1, keepdims=True)
            y = (d * lax.rsqrt(var + EPS)) * gn_g_ref[:, cols] + gn_b_ref[:, cols]
            gate = g_ref[h, rows, :].astype(F32)
            mix_out[rows, cols] = (_silu(gate) * y).astype(BF16)

    def project_rows(c, mix_in):
        rows = slice(c * CHUNK, (c + OUT_PROJ_SLAB_CHUNKS) * CHUNK)
        y = (jnp.dot(mix_in[rows, :], w_ret_ref[...], preferred_element_type=F32)
             + jnp.dot(sb_ref[rows, :], w_sb_ref[...], preferred_element_type=F32))
        xn = x_ref[rows, :] + y
        if final_norm:
            ms = jnp.mean(xn * xn, axis=-1, keepdims=True)
            xn = (xn * lax.rsqrt(ms + EPS)) * fgain_ref[...]
        out_ref[rows, :] = xn

    def step(mix_out, mix_in):
        for c0 in range(0, n_chunks, OUT_PROJ_SLAB_CHUNKS):
            for c in range(c0, c0 + OUT_PROJ_SLAB_CHUNKS):
                retention_chunk(c, mix_out)
            project_rows(c0, mix_in)

    @pl.when(s % 2 == 0)
    def _():
        step(mix_a, mix_b)

    @pl.when(s % 2 == 1)
    def _():
        step(mix_b, mix_a)


def _ret_out_proj(qkvg, tables, gn_gain, gn_bias, mix_sb, x2d, w_out, fgain, final_norm,
                  seq, n_heads, *, tm=512):
    decay, xi_b, zeta_b, gc_b = tables
    n_rows, d_model = x2d.shape
    ret_w = n_heads * HEAD_DIM
    sb_w = mix_sb.shape[1]
    n_tiles = n_rows // tm
    assert (tm // CHUNK) % OUT_PROJ_SLAB_CHUNKS == 0 and n_heads % RETENTION_HEADS_AT_ONCE == 0
    assert ret_w == sb_w and w_out.shape[0] == ret_w + sb_w
    def slab(group):
        return pl.BlockSpec((n_heads, tm, HEAD_DIM),
                            lambda s, group=group: (group, jnp.minimum(s, n_tiles - 1), 0))
    whole = lambda s: (0, 0, 0)
    prev = lambda s: (jnp.maximum(s - 1, 0), 0)
    kernel = functools.partial(_ret_out_proj_kernel, final_norm=final_norm, tiles_per_seq=seq // tm)
    return pl.pallas_call(
        kernel,
        grid=(n_tiles + 1,),
        in_specs=[
            slab(0), slab(1), slab(2), slab(3),
            pl.BlockSpec((n_heads, CHUNK, CHUNK), whole),
            pl.BlockSpec((n_heads, CHUNK, HEAD_DIM), whole),
            pl.BlockSpec((n_heads, CHUNK, HEAD_DIM), whole),
            pl.BlockSpec((n_heads, 1, HEAD_DIM), whole),
            pl.BlockSpec((1, ret_w), lambda s: (0, 0)),
            pl.BlockSpec((1, ret_w), lambda s: (0, 0)),
            pl.BlockSpec((tm, sb_w), prev),
            pl.BlockSpec((tm, d_model), prev),
            pl.BlockSpec((ret_w, d_model), lambda s: (0, 0)),
            pl.BlockSpec((sb_w, d_model), lambda s: (1, 0)),
            pl.BlockSpec((1, d_model), lambda s: (0, 0)),
        ],
        out_specs=pl.BlockSpec((tm, d_model), prev),
        out_shape=jax.ShapeDtypeStruct((n_rows, d_model), F32),
        scratch_shapes=[pltpu.VMEM((n_heads, HEAD_DIM, HEAD_DIM), F32),
                        pltpu.VMEM((tm, ret_w), BF16),
                        pltpu.VMEM((tm, ret_w), BF16)],
        compiler_params=pltpu.CompilerParams(
            dimension_semantics=("arbitrary",),
            vmem_limit_bytes=VMEM_LIMIT_BYTES),
        name="ret_out_proj",
    )(qkvg, qkvg, qkvg, qkvg, decay, xi_b, zeta_b, gc_b, gn_gain, gn_bias,
      mix_sb, x2d, w_out, w_out, fgain)


def _retention_tables(n_heads):
    lg = jnp.log1p(-jnp.exp2(-5.0 - jnp.arange(n_heads, dtype=F32)))
    n = jnp.arange(CHUNK, dtype=F32)
    rel = n[:, None] - n[None, :]
    decay = jnp.where(rel >= 0, jnp.exp(lg[:, None, None] * jnp.maximum(rel, 0.0)), 0.0)
    xi = jnp.exp(lg[:, None] * (n + 1.0))
    zeta = jnp.exp(lg[:, None] * (CHUNK - 1.0 - n))
    gamma_c = jnp.exp(lg * CHUNK)
    xi_b = jnp.broadcast_to(xi[:, :, None], (n_heads, CHUNK, HEAD_DIM))
    zeta_b = jnp.broadcast_to(zeta[:, :, None], (n_heads, CHUNK, HEAD_DIM))
    gc_b = jnp.broadcast_to(gamma_c[:, None, None], (n_heads, 1, HEAD_DIM))
    return decay, xi_b, zeta_b, gc_b


def kernel(x, norm_gain, w_in, ret_gn_gain, ret_gn_bias, sb_norm_gain, w_out, final_norm_gain):
    batch, seq, d_model = x.shape
    depth = norm_gain.shape[0]
    ret_width = ret_gn_gain.shape[1]
    sb_width = sb_norm_gain.shape[1]
    ret_heads = ret_width // HEAD_DIM
    sb_heads = sb_width // HEAD_DIM
    assert ret_heads == sb_heads and w_in.shape[2] == N_GROUPS * ret_width

    cos_t, sin_t = _epilogue_tables(seq)
    tables = _retention_tables(ret_heads)
    fgain = final_norm_gain.reshape(1, d_model)
    x2d = x.reshape(batch * seq, d_model)
    for l in range(depth):
        qkvg = _in_proj(x2d, norm_gain[l].reshape(1, d_model), w_in[l].astype(BF16), cos_t, sin_t, seq)
        mix_sb = _stickbreak(qkvg, sb_norm_gain[l].reshape(1, sb_width), batch, seq, sb_heads, 4)
        x2d = _ret_out_proj(qkvg, tables, ret_gn_gain[l].reshape(1, ret_width),
                            ret_gn_bias[l].reshape(1, ret_width), mix_sb, x2d,
                            w_out[l].astype(BF16), fgain,
                            (l == depth - 1), seq, ret_heads)
    return x2d.reshape(batch, seq, d_model)
```

```python
import functools

import jax
import jax.numpy as jnp
from jax import lax
from jax.experimental import pallas as pl
from jax.experimental.pallas import tpu as pltpu

HEAD_DIM = 128
CHUNK = 128
ROPE_THETA = 10000.0
EPS = 1e-6
LOG2E = 1.4426950408889634
MASKED_LOGIT = -1e30
UNDERFLOW_LOG = 105.0
N_GROUPS = 8
IN_PROJ_ROW_SLAB = 256
RETENTION_HEADS_AT_ONCE = 4
OUT_PROJ_SLAB_CHUNKS = 2

F32 = jnp.float32
BF16 = jnp.bfloat16

VMEM_LIMIT_BYTES = 60 * 1024 * 1024


def _nt_dot(a, b):
    return lax.dot_general(a, b, (((1,), (1,)), ((), ())), preferred_element_type=F32)


def _tn_dot(a, b):
    return lax.dot_general(a, b, (((0,), (0,)), ((), ())), preferred_element_type=F32)


def _silu(g):
    return (0.5 * g) * (1.0 + jnp.tanh(0.5 * g))


EPILOGUE_ROTARY, EPILOGUE_ROTARY_SCALED, EPILOGUE_SCALED, EPILOGUE_IDENTITY = range(4)
GROUP_EPILOGUE = (EPILOGUE_ROTARY, EPILOGUE_ROTARY_SCALED, EPILOGUE_IDENTITY, EPILOGUE_IDENTITY,
                  EPILOGUE_SCALED, EPILOGUE_IDENTITY, EPILOGUE_IDENTITY, EPILOGUE_IDENTITY)


def _in_proj_kernel(kind_tab, x_ref, gain_ref, w_ref, *refs):
    n_tables = (len(refs) - 2) // 2
    cos_refs, sin_refs = refs[:n_tables], refs[n_tables:2 * n_tables]
    out_ref, h_ref = refs[2 * n_tables:]
    heads_per_group = out_ref.shape[0] // n_tables

    @pl.when(pl.program_id(1) == 0)
    def _():
        xf = x_ref[...]
        ms = jnp.mean(xf * xf, axis=-1, keepdims=True)
        h_ref[...] = ((xf * lax.rsqrt(ms + EPS)) * gain_ref[...]).astype(BF16)

    def project(with_sin):
        w = w_ref[...]
        for r in range(0, h_ref.shape[0], IN_PROJ_ROW_SLAB):
            rows = slice(r, r + IN_PROJ_ROW_SLAB)
            acc = jnp.dot(h_ref[rows, :], w, preferred_element_type=F32)
            for c in range(out_ref.shape[0]):
                blk = acc[:, c * HEAD_DIM:(c + 1) * HEAD_DIM]
                y = blk * cos_refs[c // heads_per_group][0, rows, :]
                if with_sin:
                    y = y + pltpu.roll(blk, HEAD_DIM // 2, 1) * sin_refs[c // heads_per_group][0, rows, :]
                out_ref[c, rows, :] = y.astype(BF16)

    j = pl.program_id(1)
    has_sin = kind_tab[j * n_tables] <= EPILOGUE_ROTARY_SCALED
    for g in range(1, n_tables):
        has_sin = has_sin | (kind_tab[j * n_tables + g] <= EPILOGUE_ROTARY_SCALED)
    pl.when(has_sin)(functools.partial(project, True))
    pl.when(jnp.logical_not(has_sin))(functools.partial(project, False))


def _epilogue_tables(seq):
    half = HEAD_DIM // 2
    inv = ROPE_THETA ** (-jnp.arange(half, dtype=F32) / half)
    ang = jnp.arange(seq, dtype=F32)[:, None] * inv[None, :]
    cos = jnp.concatenate([jnp.cos(ang)] * 2, axis=-1)
    sin = jnp.concatenate([-jnp.sin(ang), jnp.sin(ang)], axis=-1)
    ones = jnp.ones_like(cos)
    zeros = jnp.zeros_like(sin)
    scale = HEAD_DIM ** -0.5
    return (jnp.stack([cos, cos * scale, ones * scale, ones]),
            jnp.stack([sin, sin * scale, zeros, zeros]))


def _in_proj(x2d, gain, w_bf, cos_t, sin_t, seq, *, tm=1024, groups_per_tile=2):
    n_rows, d_model = x2d.shape
    in_cols = w_bf.shape[1]
    tn = groups_per_tile * (in_cols // N_GROUPS)
    heads_per_tile = tn // HEAD_DIM
    seq_tiles = seq // tm
    assert tm % IN_PROJ_ROW_SLAB == 0 and N_GROUPS % groups_per_tile == 0
    def table(g):
        return pl.BlockSpec((1, tm, HEAD_DIM),
                            lambda i, j, kind: (kind[j * groups_per_tile + g], i % seq_tiles, 0))
    tables = [table(g) for g in range(groups_per_tile)]
    grid_spec = pltpu.PrefetchScalarGridSpec(
        num_scalar_prefetch=1,
        grid=(n_rows // tm, N_GROUPS // groups_per_tile),
        in_specs=[
            pl.BlockSpec((tm, d_model), lambda i, j, kind: (i, 0)),
            pl.BlockSpec((1, d_model), lambda i, j, kind: (0, 0)),
            pl.BlockSpec((d_model, tn), lambda i, j, kind: (0, j)),
        ] + tables + tables,
        out_specs=pl.BlockSpec((heads_per_tile, tm, HEAD_DIM), lambda i, j, kind: (j, i, 0)),
        scratch_shapes=[pltpu.VMEM((tm, d_model), BF16)],
    )
    return pl.pallas_call(
        _in_proj_kernel,
        grid_spec=grid_spec,
        out_shape=jax.ShapeDtypeStruct((in_cols // HEAD_DIM, n_rows, HEAD_DIM), BF16),
        compiler_params=pltpu.CompilerParams(
            dimension_semantics=("arbitrary", "arbitrary"),
            vmem_limit_bytes=VMEM_LIMIT_BYTES),
        name="in_proj",
    )(jnp.asarray(GROUP_EPILOGUE, jnp.int32), x2d, gain, w_bf,
      *([cos_t] * groups_per_tile), *([sin_t] * groups_per_tile))


def _stickbreak_kernel(q_ref, k_ref, v_ref, g_ref, gain_ref, out_ref,
                       acc_ref, carry_ref, tri_ref, bias_ref, z_a, sp_a, z_b, sp_b, *, blk, width):
    row = lax.broadcasted_iota(jnp.int32, (blk, blk), 0)
    col = lax.broadcasted_iota(jnp.int32, (blk, blk), 1)
    tri_ref[...] = (row >= col).astype(BF16)
    bias_ref[...] = jnp.where(col < row, 0.0, MASKED_LOGIT)
    for hd in range(q_ref.shape[0]):
        one = pl.ds(hd, 1)
        cols = pl.ds(hd * HEAD_DIM, HEAD_DIM)
        _stickbreak_head(q_ref.at[one], k_ref.at[one], v_ref.at[one], g_ref.at[one],
                         gain_ref.at[:, cols], out_ref.at[:, cols], acc_ref.at[hd], carry_ref.at[hd],
                         tri_ref, bias_ref, z_a, sp_a, z_b, sp_b, blk=blk, width=width)


def _stickbreak_head(q_ref, k_ref, v_ref, g_ref, gain_ref, out_ref,
                     acc_ref, carry_ref, tri_ref, bias_ref, z_a, sp_a, z_b, sp_b, *, blk, width):
    seq = q_ref.shape[1]
    n_blk = seq // blk
    n_groups = n_blk // width

    def rows(start, size):
        return pl.ds(start if isinstance(start, int) else pl.multiple_of(start, blk), size)

    def scores(tiles, z_buf, sp_buf):
        for slot, r0, q0, m, k0, n, diagonal, _ in tiles:
            z = _nt_dot(q_ref[0, rows(q0, m), :], k_ref[0, rows(k0, n), :])
            if diagonal:
                z = z + bias_ref[r0:r0 + m, 0:n]
            sp = jnp.maximum(z, 0.0) + jnp.log(1.0 + jnp.exp2(jnp.abs(z) * (-LOG2E)))
            z_buf[slot, r0:r0 + m, 0:n] = z
            sp_buf[slot, r0:r0 + m, 0:n] = sp.astype(BF16)

    def weights(tiles, z_buf, sp_buf, first):
        csums = [jnp.dot(sp_buf[slot, r0:r0 + m, 0:n], tri_ref[0:n, 0:n], preferred_element_type=F32)
                 for slot, r0, _, m, _, n, _, _ in tiles]
        if not first:
            carries = [carry_ref[rows(q0, m), :] for _, _, q0, m, _, _, _, _ in tiles]
            csums_c = [c + jnp.concatenate([carry] * (c.shape[1] // HEAD_DIM), axis=1)
                       for c, carry in zip(csums, carries)]
        else:
            csums_c = csums
        pvs = [jnp.dot(jnp.exp(z_buf[slot, r0:r0 + m, 0:n] - c).astype(BF16), v_ref[0, rows(k0, n), :],
                       preferred_element_type=F32)
               for (slot, r0, _, m, k0, n, _, _), c in zip(tiles, csums_c)]
        if not first:
            accs = [acc_ref[rows(q0, m), :] for _, _, q0, m, _, _, _, _ in tiles]
        for t, (_, _, q0, m, _, _, _, weight) in enumerate(tiles):
            total = jnp.broadcast_to(csums[t][:, 0:1], (m, HEAD_DIM))
            pv = pvs[t]
            if weight is not None:
                total = weight * total
                pv = weight * pv
            carry_ref[rows(q0, m), :] = total if first else carries[t] + total
            acc_ref[rows(q0, m), :] = pv if first else accs[t] + pv

    buffers = ((z_a, sp_a), (z_b, sp_b))
    half = blk // 2

    def diagonal_tiles(slot, i):
        return [(slot, 0, i * blk, half, i * blk, half, True, None),
                (slot, half, i * blk + half, half, i * blk, blk, True, None)]

    near = ([[tile for t in range(width) for tile in diagonal_tiles(t, n * width + t)]
             for n in range(n_groups)]
            + [[(i % width, 0, i * blk, blk, (i - 1) * blk, blk, False, None)
                for i in range(n * width, (n + 1) * width) if i >= 1] for n in range(n_groups)])
    scores(near[0], *buffers[0])
    for g, tiles in enumerate(near):
        if g + 1 < len(near):
            scores(near[g + 1], *buffers[(g + 1) % 2])
        weights(tiles, *buffers[g % 2], first=(g < n_groups))

    def group_live(n, d):
        live = jnp.bool_(False)
        for i in range(n * width, (n + 1) * width):
            live = live | ((i >= d) & (jnp.min(carry_ref[rows(i * blk, blk), :]) < UNDERFLOW_LOG))
        return live

    def any_live(d):
        live = jnp.bool_(False)
        for n in range(n_groups):
            live = live | group_live(n, d)
        return live

    def far_wave(d):
        for n in range(n_groups):
            @pl.when(group_live(n, d))
            def _():
                tiles = [(i % width, 0, i * blk, blk, jnp.maximum(i - d, 0) * blk, blk, False,
                          (i >= d).astype(F32)) for i in range(n * width, (n + 1) * width)]
                scores(tiles, *buffers[0])
                weights(tiles, *buffers[0], first=False)
        return d + 1

    lax.while_loop(lambda d: (d < n_blk) & any_live(d), far_wave, jnp.int32(2))

    def finish(i, _):
        block = rows(i * blk, blk)
        acc = acc_ref[block, :]
        ms = jnp.mean(acc * acc, axis=-1, keepdims=True)
        y = (acc * lax.rsqrt(ms + EPS)) * gain_ref[...]
        gate = g_ref[0, block, :].astype(F32)
        out_ref[block, :] = (_silu(gate) * y).astype(BF16)
        return 0

    lax.fori_loop(0, n_blk, finish, 0, unroll=True)


def _stickbreak(qkvg, gain, batch, seq, n_heads, first_group, *, blk=256, width=4, heads_per_step=2):
    n_rows = batch * seq
    assert (seq // blk) % width == 0 and n_heads % heads_per_step == 0
    def slab(group):
        return pl.BlockSpec(
            (heads_per_step, seq, HEAD_DIM),
            lambda b, h, group=group: (((first_group + group) * n_heads) // heads_per_step + h, b, 0))
    kernel = functools.partial(_stickbreak_kernel, blk=blk, width=width)
    stage_buffers = [pltpu.VMEM((width, blk, blk), F32),
                     pltpu.VMEM((width, blk, blk), BF16)]
    return pl.pallas_call(
        kernel,
        grid=(batch, n_heads // heads_per_step),
        in_specs=[slab(0), slab(1), slab(2), slab(3),
                  pl.BlockSpec((1, heads_per_step * HEAD_DIM), lambda b, h: (0, h))],
        out_specs=pl.BlockSpec((seq, heads_per_step * HEAD_DIM), lambda b, h: (b, h)),
        out_shape=jax.ShapeDtypeStruct((n_rows, n_heads * HEAD_DIM), BF16),
        scratch_shapes=[pltpu.VMEM((heads_per_step, seq, HEAD_DIM), F32),
                        pltpu.VMEM((heads_per_step, seq, HEAD_DIM), F32),
                        pltpu.VMEM((blk, blk), BF16),
                        pltpu.VMEM((blk, blk), F32),
                        ] + stage_buffers + stage_buffers,
        compiler_params=pltpu.CompilerParams(
            dimension_semantics=("arbitrary", "arbitrary"),
            vmem_limit_bytes=VMEM_LIMIT_BYTES),
        name="stickbreak",
    )(qkvg, qkvg, qkvg, qkvg, gain)


def _ret_out_proj_kernel(q_ref, k_ref, v_ref, g_ref, decay_ref, xi_ref, zeta_ref, gc_ref,
                         gn_g_ref, gn_b_ref, sb_ref, x_ref, w_ret_ref, w_sb_ref, fgain_ref, out_ref,
                         r_ref, mix_a, mix_b, *, final_norm, tiles_per_seq):
    s = pl.program_id(0)
    n_tiles = pl.num_programs(0) - 1
    n_heads = q_ref.shape[0]
    n_chunks = q_ref.shape[1] // CHUNK

    @pl.when(s == 0)
    def _():
        r_ref[...] = jnp.zeros_like(r_ref)
        mix_b[...] = jnp.zeros_like(mix_b)

    tile = jnp.minimum(s, n_tiles - 1)
    keep = (tile % tiles_per_seq != 0).astype(F32)

    def retention_chunk(c, mix_out):
        for h0 in range(0, n_heads, RETENTION_HEADS_AT_ONCE):
            retention_heads(c, range(h0, h0 + RETENTION_HEADS_AT_ONCE), mix_out)

    def retention_heads(c, heads, mix_out):
        rows = slice(c * CHUNK, (c + 1) * CHUNK)
        qc = {h: q_ref[h, rows, :] for h in heads}
        kc = {h: k_ref[h, rows, :] for h in heads}
        vc = {h: v_ref[h, rows, :] for h in heads}
        rs = {h: r_ref[h] * keep if c == 0 else r_ref[h] for h in heads}
        scores = {h: _nt_dot(qc[h], kc[h]) for h in heads}
        kvs = {h: _tn_dot((kc[h].astype(F32) * zeta_ref[h]).astype(BF16), vc[h]) for h in heads}
        cross = {h: jnp.dot((qc[h].astype(F32) * xi_ref[h]).astype(BF16), rs[h].astype(BF16),
                            preferred_element_type=F32) for h in heads}
        intra = {h: jnp.dot((scores[h] * decay_ref[h]).astype(BF16), vc[h], preferred_element_type=F32)
                 for h in heads}
        for h in heads:
            cols = slice(h * HEAD_DIM, (h + 1) * HEAD_DIM)
            r_ref[h] = gc_ref[h] * rs[h] + kvs[h]
            o = intra[h] + cross[h]
            mu = jnp.mean(o, axis=-1, keepdims=True)
            d = o - mu
            var = jnp.mean(d * d, axis=-1, keepdims=True)
            y = (d * lax.rsqrt(var + EPS)) * gn_g_ref[:, cols] + gn_b_ref[:, cols]
            gate = g_ref[h, rows, :].astype(F32)
            mix_out[rows, cols] = (_silu(gate) * y).astype(BF16)

    def project_rows(c, mix_in):
        rows = slice(c * CHUNK, (c + OUT_PROJ_SLAB_CHUNKS) * CHUNK)
        y = (jnp.dot(mix_in[rows, :], w_ret_ref[...], preferred_element_type=F32)
             + jnp.dot(sb_ref[rows, :], w_sb_ref[...], preferred_element_type=F32))
        xn = x_ref[rows, :] + y
        if final_norm:
            ms = jnp.mean(xn * xn, axis=-1, keepdims=True)
            xn = (xn * lax.rsqrt(ms + EPS)) * fgain_ref[...]
        out_ref[rows, :] = xn

    def step(mix_out, mix_in):
        for c0 in range(0, n_chunks, OUT_PROJ_SLAB_CHUNKS):
            for c in range(c0, c0 + OUT_PROJ_SLAB_CHUNKS):
                retention_chunk(c, mix_out)
            project_rows(c0, mix_in)

    @pl.when(s % 2 == 0)
    def _():
        step(mix_a, mix_b)

    @pl.when(s % 2 == 1)
    def _():
        step(mix_b, mix_a)


def _ret_out_proj(qkvg, tables, gn_gain, gn_bias, mix_sb, x2d, w_out, fgain, final_norm,
                  seq, n_heads, *, tm=512):
    decay, xi_b, zeta_b, gc_b = tables
    n_rows, d_model = x2d.shape
    ret_w = n_heads * HEAD_DIM
    sb_w = mix_sb.shape[1]
    n_tiles = n_rows // tm
    assert (tm // CHUNK) % OUT_PROJ_SLAB_CHUNKS == 0 and n_heads % RETENTION_HEADS_AT_ONCE == 0
    assert ret_w == sb_w and w_out.shape[0] == ret_w + sb_w
    def slab(group):
        return pl.BlockSpec((n_heads, tm, HEAD_DIM),
                            lambda s, group=group: (group, jnp.minimum(s, n_tiles - 1), 0))
    whole = lambda s: (0, 0, 0)
    prev = lambda s: (jnp.maximum(s - 1, 0), 0)
    kernel = functools.partial(_ret_out_proj_kernel, final_norm=final_norm, tiles_per_seq=seq // tm)
    return pl.pallas_call(
        kernel,
        grid=(n_tiles + 1,),
        in_specs=[
            slab(0), slab(1), slab(2), slab(3),
            pl.BlockSpec((n_heads, CHUNK, CHUNK), whole),
            pl.BlockSpec((n_heads, CHUNK, HEAD_DIM), whole),
            pl.BlockSpec((n_heads, CHUNK, HEAD_DIM), whole),
            pl.BlockSpec((n_heads, 1, HEAD_DIM), whole),
            pl.BlockSpec((1, ret_w), lambda s: (0, 0)),
            pl.BlockSpec((1, ret_w), lambda s: (0, 0)),
            pl.BlockSpec((tm, sb_w), prev),
            pl.BlockSpec((tm, d_model), prev),
            pl.BlockSpec((ret_w, d_model), lambda s: (0, 0)),
            pl.BlockSpec((sb_w, d_model), lambda s: (1, 0)),
            pl.BlockSpec((1, d_model), lambda s: (0, 0)),
        ],
        out_specs=pl.BlockSpec((tm, d_model), prev),
        out_shape=jax.ShapeDtypeStruct((n_rows, d_model), F32),
        scratch_shapes=[pltpu.VMEM((n_heads, HEAD_DIM, HEAD_DIM), F32),
                        pltpu.VMEM((tm, ret_w), BF16),
                        pltpu.VMEM((tm, ret_w), BF16)],
        compiler_params=pltpu.CompilerParams(
            dimension_semantics=("arbitrary",),
            vmem_limit_bytes=VMEM_LIMIT_BYTES),
        name="ret_out_proj",
    )(qkvg, qkvg, qkvg, qkvg, decay, xi_b, zeta_b, gc_b, gn_gain, gn_bias,
      mix_sb, x2d, w_out, w_out, fgain)


def _retention_tables(n_heads):
    lg = jnp.log1p(-jnp.exp2(-5.0 - jnp.arange(n_heads, dtype=F32)))
    n = jnp.arange(CHUNK, dtype=F32)
    rel = n[:, None] - n[None, :]
    decay = jnp.where(rel >= 0, jnp.exp(lg[:, None, None] * jnp.maximum(rel, 0.0)), 0.0)
    xi = jnp.exp(lg[:, None] * (n + 1.0))
    zeta = jnp.exp(lg[:, None] * (CHUNK - 1.0 - n))
    gamma_c = jnp.exp(lg * CHUNK)
    xi_b = jnp.broadcast_to(xi[:, :, None], (n_heads, CHUNK, HEAD_DIM))
    zeta_b = jnp.broadcast_to(zeta[:, :, None], (n_heads, CHUNK, HEAD_DIM))
    gc_b = jnp.broadcast_to(gamma_c[:, None, None], (n_heads, 1, HEAD_DIM))
    return decay, xi_b, zeta_b, gc_b


def kernel(x, norm_gain, w_in, ret_gn_gain, ret_gn_bias, sb_norm_gain, w_out, final_norm_gain):
    batch, seq, d_model = x.shape
    depth = norm_gain.shape[0]
    ret_width = ret_gn_gain.shape[1]
    sb_width = sb_norm_gain.shape[1]
    ret_heads = ret_width // HEAD_DIM
    sb_heads = sb_width // HEAD_DIM
    assert ret_heads == sb_heads and w_in.shape[2] == N_GROUPS * ret_width

    cos_t, sin_t = _epilogue_tables(seq)
    tables = _retention_tables(ret_heads)
    fgain = final_norm_gain.reshape(1, d_model)
    x2d = x.reshape(batch * seq, d_model)
    for l in range(depth):
        qkvg = _in_proj(x2d, norm_gain[l].reshape(1, d_model), w_in[l].astype(BF16), cos_t, sin_t, seq)
        mix_sb = _stickbreak(qkvg, sb_norm_gain[l].reshape(1, sb_width), batch, seq, sb_heads, 4)
        x2d = _ret_out_proj(qkvg, tables, ret_gn_gain[l].reshape(1, ret_width),
                            ret_gn_bias[l].reshape(1, ret_width), mix_sb, x2d,
                            w_out[l].astype(BF16), fgain,
                            (l == depth - 1), seq, ret_heads)
    return x2d.reshape(batch, seq, d_model)
```

```python
import functools

import jax
import jax.numpy as jnp
from jax import lax
from jax.experimental import pallas as pl
from jax.experimental.pallas import tpu as pltpu

HEAD_DIM = 128
CHUNK = 128
ROPE_THETA = 10000.0
EPS = 1e-6
LOG2E = 1.4426950408889634
MASKED_LOGIT = -1e30
UNDERFLOW_LOG = 105.0
N_GROUPS = 8
IN_PROJ_ROW_SLAB = 256
RETENTION_HEADS_AT_ONCE = 4
OUT_PROJ_SLAB_CHUNKS = 2

F32 = jnp.float32
BF16 = jnp.bfloat16

VMEM_LIMIT_BYTES = 60 * 1024 * 1024


def _nt_dot(a, b):
    return lax.dot_general(a, b, (((1,), (1,)), ((), ())), preferred_element_type=F32)


def _tn_dot(a, b):
    return lax.dot_general(a, b, (((0,), (0,)), ((), ())), preferred_element_type=F32)


def _silu(g):
    return (0.5 * g) * (1.0 + jnp.tanh(0.5 * g))


EPILOGUE_ROTARY, EPILOGUE_ROTARY_SCALED, EPILOGUE_SCALED, EPILOGUE_IDENTITY = range(4)
GROUP_EPILOGUE = (EPILOGUE_ROTARY, EPILOGUE_ROTARY_SCALED, EPILOGUE_IDENTITY, EPILOGUE_IDENTITY,
                  EPILOGUE_SCALED, EPILOGUE_IDENTITY, EPILOGUE_IDENTITY, EPILOGUE_IDENTITY)


def _in_proj_kernel(kind_tab, x_ref, gain_ref, w_ref, *refs):
    n_tables = (len(refs) - 2) // 2
    cos_refs, sin_refs = refs[:n_tables], refs[n_tables:2 * n_tables]
    out_ref, h_ref = refs[2 * n_tables:]
    heads_per_group = out_ref.shape[0] // n_tables

    @pl.when(pl.program_id(1) == 0)
    def _():
        xf = x_ref[...]
        ms = jnp.mean(xf * xf, axis=-1, keepdims=True)
        h_ref[...] = ((xf * lax.rsqrt(ms + EPS)) * gain_ref[...]).astype(BF16)

    def project(with_sin):
        w = w_ref[...]
        for r in range(0, h_ref.shape[0], IN_PROJ_ROW_SLAB):
            rows = slice(r, r + IN_PROJ_ROW_SLAB)
            acc = jnp.dot(h_ref[rows, :], w, preferred_element_type=F32)
            for c in range(out_ref.shape[0]):
                blk = acc[:, c * HEAD_DIM:(c + 1) * HEAD_DIM]
                y = blk * cos_refs[c // heads_per_group][0, rows, :]
                if with_sin:
                    y = y + pltpu.roll(blk, HEAD_DIM // 2, 1) * sin_refs[c // heads_per_group][0, rows, :]
                out_ref[c, rows, :] = y.astype(BF16)

    j = pl.program_id(1)
    has_sin = kind_tab[j * n_tables] <= EPILOGUE_ROTARY_SCALED
    for g in range(1, n_tables):
        has_sin = has_sin | (kind_tab[j * n_tables + g] <= EPILOGUE_ROTARY_SCALED)
    pl.when(has_sin)(functools.partial(project, True))
    pl.when(jnp.logical_not(has_sin))(functools.partial(project, False))


def _epilogue_tables(seq):
    half = HEAD_DIM // 2
    inv = ROPE_THETA ** (-jnp.arange(half, dtype=F32) / half)
    ang = jnp.arange(seq, dtype=F32)[:, None] * inv[None, :]
    cos = jnp.concatenate([jnp.cos(ang)] * 2, axis=-1)
    sin = jnp.concatenate([-jnp.sin(ang), jnp.sin(ang)], axis=-1)
    ones = jnp.ones_like(cos)
    zeros = jnp.zeros_like(sin)
    scale = HEAD_DIM ** -0.5
    return (jnp.stack([cos, cos * scale, ones * scale, ones]),
            jnp.stack([sin, sin * scale, zeros, zeros]))


def _in_proj(x2d, gain, w_bf, cos_t, sin_t, seq, *, tm=1024, groups_per_tile=2):
    n_rows, d_model = x2d.shape
    in_cols = w_bf.shape[1]
    tn = groups_per_tile * (in_cols // N_GROUPS)
    heads_per_tile = tn // HEAD_DIM
    seq_tiles = seq // tm
    assert tm % IN_PROJ_ROW_SLAB == 0 and N_GROUPS % groups_per_tile == 0
    def table(g):
        return pl.BlockSpec((1, tm, HEAD_DIM),
                            lambda i, j, kind: (kind[j * groups_per_tile + g], i % seq_tiles, 0))
    tables = [table(g) for g in range(groups_per_tile)]
    grid_spec = pltpu.PrefetchScalarGridSpec(
        num_scalar_prefetch=1,
        grid=(n_rows // tm, N_GROUPS // groups_per_tile),
        in_specs=[
            pl.BlockSpec((tm, d_model), lambda i, j, kind: (i, 0)),
            pl.BlockSpec((1, d_model), lambda i, j, kind: (0, 0)),
            pl.BlockSpec((d_model, tn), lambda i, j, kind: (0, j)),
        ] + tables + tables,
        out_specs=pl.BlockSpec((heads_per_tile, tm, HEAD_DIM), lambda i, j, kind: (j, i, 0)),
        scratch_shapes=[pltpu.VMEM((tm, d_model), BF16)],
    )
    return pl.pallas_call(
        _in_proj_kernel,
        grid_spec=grid_spec,
        out_shape=jax.ShapeDtypeStruct((in_cols // HEAD_DIM, n_rows, HEAD_DIM), BF16),
        compiler_params=pltpu.CompilerParams(
            dimension_semantics=("arbitrary", "arbitrary"),
            vmem_limit_bytes=VMEM_LIMIT_BYTES),
        name="in_proj",
    )(jnp.asarray(GROUP_EPILOGUE, jnp.int32), x2d, gain, w_bf,
      *([cos_t] * groups_per_tile), *([sin_t] * groups_per_tile))


def _stickbreak_kernel(q_ref, k_ref, v_ref, g_ref, gain_ref, out_ref,
                       acc_ref, carry_ref, tri_ref, bias_ref, z_a, sp_a, z_b, sp_b, *, blk, width):
    seq = q_ref.shape[1]
    n_blk = seq // blk
    n_groups = n_blk // width
    row = lax.broadcasted_iota(jnp.int32, (blk, blk), 0)
    col = lax.broadcasted_iota(jnp.int32, (blk, blk), 1)
    tri_ref[...] = (row >= col).astype(BF16)
    bias_ref[...] = jnp.where(col < row, 0.0, MASKED_LOGIT)

    def rows(start, size):
        return pl.ds(start if isinstance(start, int) else pl.multiple_of(start, blk), size)

    def scores(tiles, z_buf, sp_buf):
        for slot, r0, q0, m, k0, n, diagonal, _ in tiles:
            z = _nt_dot(q_ref[0, rows(q0, m), :], k_ref[0, rows(k0, n), :])
            if diagonal:
                z = z + bias_ref[r0:r0 + m, 0:n]
            sp = jnp.maximum(z, 0.0) + jnp.log(1.0 + jnp.exp2(jnp.abs(z) * (-LOG2E)))
            z_buf[slot, r0:r0 + m, 0:n] = z
            sp_buf[slot, r0:r0 + m, 0:n] = sp.astype(BF16)

    def weights(tiles, z_buf, sp_buf, first):
        csums = [jnp.dot(sp_buf[slot, r0:r0 + m, 0:n], tri_ref[0:n, 0:n], preferred_element_type=F32)
                 for slot, r0, _, m, _, n, _, _ in tiles]
        if not first:
            carries = [carry_ref[rows(q0, m), :] for _, _, q0, m, _, _, _, _ in tiles]
            csums_c = [c + jnp.concatenate([carry] * (c.shape[1] // HEAD_DIM), axis=1)
                       for c, carry in zip(csums, carries)]
        else:
            csums_c = csums
        pvs = [jnp.dot(jnp.exp(z_buf[slot, r0:r0 + m, 0:n] - c).astype(BF16), v_ref[0, rows(k0, n), :],
                       preferred_element_type=F32)
               for (slot, r0, _, m, k0, n, _, _), c in zip(tiles, csums_c)]
        if not first:
            accs = [acc_ref[rows(q0, m), :] for _, _, q0, m, _, _, _, _ in tiles]
        for t, (_, _, q0, m, _, _, _, weight) in enumerate(tiles):
            total = jnp.broadcast_to(csums[t][:, 0:1], (m, HEAD_DIM))
            pv = pvs[t]
            if weight is not None:
                total = weight * total
                pv = weight * pv
            carry_ref[rows(q0, m), :] = total if first else carries[t] + total
            acc_ref[rows(q0, m), :] = pv if first else accs[t] + pv

    buffers = ((z_a, sp_a), (z_b, sp_b))
    half = blk // 2

    def diagonal_tiles(slot, i):
        return [(slot, 0, i * blk, half, i * blk, half, True, None),
                (slot, half, i * blk + half, half, i * blk, blk, True, None)]

    near = ([[tile for t in range(width) for tile in diagonal_tiles(t, n * width + t)]
             for n in range(n_groups)]
            + [[(i % width, 0, i * blk, blk, (i - 1) * blk, blk, False, None)
                for i in range(n * width, (n + 1) * width) if i >= 1] for n in range(n_groups)])
    scores(near[0], *buffers[0])
    for g, tiles in enumerate(near):
        if g + 1 < len(near):
            scores(near[g + 1], *buffers[(g + 1) % 2])
        weights(tiles, *buffers[g % 2], first=(g < n_groups))

    def group_live(n, d):
        live = jnp.bool_(False)
        for i in range(n * width, (n + 1) * width):
            live = live | ((i >= d) & (jnp.min(carry_ref[rows(i * blk, blk), :]) < UNDERFLOW_LOG))
        return live

    def any_live(d):
        live = jnp.bool_(False)
        for n in range(n_groups):
            live = live | group_live(n, d)
        return live

    def far_wave(d):
        for n in range(n_groups):
            @pl.when(group_live(n, d))
            def _():
                tiles = [(i % width, 0, i * blk, blk, jnp.maximum(i - d, 0) * blk, blk, False,
                          (i >= d).astype(F32)) for i in range(n * width, (n + 1) * width)]
                scores(tiles, *buffers[0])
                weights(tiles, *buffers[0], first=False)
        return d + 1

    lax.while_loop(lambda d: (d < n_blk) & any_live(d), far_wave, jnp.int32(2))

    def finish(i, _):
        block = rows(i * blk, blk)
        acc = acc_ref[block, :]
        ms = jnp.mean(acc * acc, axis=-1, keepdims=True)
        y = (acc * lax.rsqrt(ms + EPS)) * gain_ref[...]
        gate = g_ref[0, block, :].astype(F32)
        out_ref[block, :] = (_silu(gate) * y).astype(BF16)
        return 0

    lax.fori_loop(0, n_blk, finish, 0, unroll=True)


def _stickbreak(qkvg, gain, batch, seq, n_heads, first_group, *, blk=256, width=4):
    n_rows = batch * seq
    assert (seq // blk) % width == 0
    def slab(group):
        return pl.BlockSpec((1, seq, HEAD_DIM),
                            lambda b, h, group=group: ((first_group + group) * n_heads + h, b, 0))
    kernel = functools.partial(_stickbreak_kernel, blk=blk, width=width)
    stage_buffers = [pltpu.VMEM((width, blk, blk), F32),
                     pltpu.VMEM((width, blk, blk), BF16)]
    return pl.pallas_call(
        kernel,
        grid=(batch, n_heads),
        in_specs=[slab(0), slab(1), slab(2), slab(3),
                  pl.BlockSpec((1, HEAD_DIM), lambda b, h: (0, h))],
        out_specs=pl.BlockSpec((seq, HEAD_DIM), lambda b, h: (b, h)),
        out_shape=jax.ShapeDtypeStruct((n_rows, n_heads * HEAD_DIM), BF16),
        scratch_shapes=[pltpu.VMEM((seq, HEAD_DIM), F32),
                        pltpu.VMEM((seq, HEAD_DIM), F32),
                        pltpu.VMEM((blk, blk), BF16),
                        pltpu.VMEM((blk, blk), F32),
                        ] + stage_buffers + stage_buffers,
        compiler_params=pltpu.CompilerParams(
            dimension_semantics=("arbitrary", "arbitrary"),
            vmem_limit_bytes=VMEM_LIMIT_BYTES),
        name="stickbreak",
    )(qkvg, qkvg, qkvg, qkvg, gain)


def _ret_out_proj_kernel(q_ref, k_ref, v_ref, g_ref, decay_ref, xi_ref, zeta_ref, gc_ref,
                         gn_g_ref, gn_b_ref, sb_ref, x_ref, w_ret_ref, w_sb_ref, fgain_ref, out_ref,
                         r_ref, mix_a, mix_b, *, final_norm, tiles_per_seq):
    s = pl.program_id(0)
    n_tiles = pl.num_programs(0) - 1
    n_heads = q_ref.shape[0]
    n_chunks = q_ref.shape[1] // CHUNK

    @pl.when(s == 0)
    def _():
        r_ref[...] = jnp.zeros_like(r_ref)
        mix_b[...] = jnp.zeros_like(mix_b)

    tile = jnp.minimum(s, n_tiles - 1)
    keep = (tile % tiles_per_seq != 0).astype(F32)

    def retention_chunk(c, mix_out):
        for h0 in range(0, n_heads, RETENTION_HEADS_AT_ONCE):
            retention_heads(c, range(h0, h0 + RETENTION_HEADS_AT_ONCE), mix_out)

    def retention_heads(c, heads, mix_out):
        rows = slice(c * CHUNK, (c + 1) * CHUNK)
        qc = {h: q_ref[h, rows, :] for h in heads}
        kc = {h: k_ref[h, rows, :] for h in heads}
        vc = {h: v_ref[h, rows, :] for h in heads}
        rs = {h: r_ref[h] * keep if c == 0 else r_ref[h] for h in heads}
        scores = {h: _nt_dot(qc[h], kc[h]) for h in heads}
        kvs = {h: _tn_dot((kc[h].astype(F32) * zeta_ref[h]).astype(BF16), vc[h]) for h in heads}
        cross = {h: jnp.dot((qc[h].astype(F32) * xi_ref[h]).astype(BF16), rs[h].astype(BF16),
                            preferred_element_type=F32) for h in heads}
        intra = {h: jnp.dot((scores[h] * decay_ref[h]).astype(BF16), vc[h], preferred_element_type=F32)
                 for h in heads}
        for h in heads:
            cols = slice(h * HEAD_DIM, (h + 1) * HEAD_DIM)
            r_ref[h] = gc_ref[h] * rs[h] + kvs[h]
            o = intra[h] + cross[h]
            mu = jnp.mean(o, axis=-1, keepdims=True)
            d = o - mu
            var = jnp.mean(d * d, axis=-1, keepdims=True)
            y = (d * lax.rsqrt(var + EPS)) * gn_g_ref[:, cols] + gn_b_ref[:, cols]
            gate = g_ref[h, rows, :].astype(F32)
            mix_out[rows, cols] = (_silu(gate) * y).astype(BF16)

    def project_rows(c, mix_in):
        rows = slice(c * CHUNK, (c + OUT_PROJ_SLAB_CHUNKS) * CHUNK)
        y = (jnp.dot(mix_in[rows, :], w_ret_ref[...], preferred_element_type=F32)
             + jnp.dot(sb_ref[rows, :], w_sb_ref[...], preferred_element_type=F32))
        xn = x_ref[rows, :] + y
        if final_norm:
            ms = jnp.mean(xn * xn, axis=-1, keepdims=True)
            xn = (xn * lax.rsqrt(ms + EPS)) * fgain_ref[...]
        out_ref[rows, :] = xn

    def step(mix_out, mix_in):
        for c0 in range(0, n_chunks, OUT_PROJ_SLAB_CHUNKS):
            for c in range(c0, c0 + OUT_PROJ_SLAB_CHUNKS):
                retention_chunk(c, mix_out)
            project_rows(c0, mix_in)

    @pl.when(s % 2 == 0)
    def _():
        step(mix_a, mix_b)

    @pl.when(s % 2 == 1)
    def _():
        step(mix_b, mix_a)


def _ret_out_proj(qkvg, tables, gn_gain, gn_bias, mix_sb, x2d, w_out, fgain, final_norm,
                  seq, n_heads, *, tm=512):
    decay, xi_b, zeta_b, gc_b = tables
    n_rows, d_model = x2d.shape
    ret_w = n_heads * HEAD_DIM
    sb_w = mix_sb.shape[1]
    n_tiles = n_rows // tm
    assert (tm // CHUNK) % OUT_PROJ_SLAB_CHUNKS == 0 and n_heads % RETENTION_HEADS_AT_ONCE == 0
    assert ret_w == sb_w and w_out.shape[0] == ret_w + sb_w
    def slab(group):
        return pl.BlockSpec((n_heads, tm, HEAD_DIM),
                            lambda s, group=group: (group, jnp.minimum(s, n_tiles - 1), 0))
    whole = lambda s: (0, 0, 0)
    prev = lambda s: (jnp.maximum(s - 1, 0), 0)
    kernel = functools.partial(_ret_out_proj_kernel, final_norm=final_norm, tiles_per_seq=seq // tm)
    return pl.pallas_call(
        kernel,
        grid=(n_tiles + 1,),
        in_specs=[
            slab(0), slab(1), slab(2), slab(3),
            pl.BlockSpec((n_heads, CHUNK, CHUNK), whole),
            pl.BlockSpec((n_heads, CHUNK, HEAD_DIM), whole),
            pl.BlockSpec((n_heads, CHUNK, HEAD_DIM), whole),
            pl.BlockSpec((n_heads, 1, HEAD_DIM), whole),
            pl.BlockSpec((1, ret_w), lambda s: (0, 0)),
            pl.BlockSpec((1, ret_w), lambda s: (0, 0)),
            pl.BlockSpec((tm, sb_w), prev),
            pl.BlockSpec((tm, d_model), prev),
            pl.BlockSpec((ret_w, d_model), lambda s: (0, 0)),
            pl.BlockSpec((sb_w, d_model), lambda s: (1, 0)),
            pl.BlockSpec((1, d_model), lambda s: (0, 0)),
        ],
        out_specs=pl.BlockSpec((tm, d_model), prev),
        out_shape=jax.ShapeDtypeStruct((n_rows, d_model), F32),
        scratch_shapes=[pltpu.VMEM((n_heads, HEAD_DIM, HEAD_DIM), F32),
                        pltpu.VMEM((tm, ret_w), BF16),
                        pltpu.VMEM((tm, ret_w), BF16)],
        compiler_params=pltpu.CompilerParams(
            dimension_semantics=("arbitrary",),
            vmem_limit_bytes=VMEM_LIMIT_BYTES),
        name="ret_out_proj",
    )(qkvg, qkvg, qkvg, qkvg, decay, xi_b, zeta_b, gc_b, gn_gain, gn_bias,
      mix_sb, x2d, w_out, w_out, fgain)


def _retention_tables(n_heads):
    lg = jnp.log1p(-jnp.exp2(-5.0 - jnp.arange(n_heads, dtype=F32)))
    n = jnp.arange(CHUNK, dtype=F32)
    rel = n[:, None] - n[None, :]
    decay = jnp.where(rel >= 0, jnp.exp(lg[:, None, None] * jnp.maximum(rel, 0.0)), 0.0)
    xi = jnp.exp(lg[:, None] * (n + 1.0))
    zeta = jnp.exp(lg[:, None] * (CHUNK - 1.0 - n))
    gamma_c = jnp.exp(lg * CHUNK)
    xi_b = jnp.broadcast_to(xi[:, :, None], (n_heads, CHUNK, HEAD_DIM))
    zeta_b = jnp.broadcast_to(zeta[:, :, None], (n_heads, CHUNK, HEAD_DIM))
    gc_b = jnp.broadcast_to(gamma_c[:, None, None], (n_heads, 1, HEAD_DIM))
    return decay, xi_b, zeta_b, gc_b


def kernel(x, norm_gain, w_in, ret_gn_gain, ret_gn_bias, sb_norm_gain, w_out, final_norm_gain):
    batch, seq, d_model = x.shape
    depth = norm_gain.shape[0]
    ret_width = ret_gn_gain.shape[1]
    sb_width = sb_norm_gain.shape[1]
    ret_heads = ret_width // HEAD_DIM
    sb_heads = sb_width // HEAD_DIM
    assert ret_heads == sb_heads and w_in.shape[2] == N_GROUPS * ret_width

    cos_t, sin_t = _epilogue_tables(seq)
    tables = _retention_tables(ret_heads)
    fgain = final_norm_gain.reshape(1, d_model)
    x2d = x.reshape(batch * seq, d_model)
    for l in range(depth):
        qkvg = _in_proj(x2d, norm_gain[l].reshape(1, d_model), w_in[l].astype(BF16), cos_t, sin_t, seq)
        mix_sb = _stickbreak(qkvg, sb_norm_gain[l].reshape(1, sb_width), batch, seq, sb_heads, 4)
        x2d = _ret_out_proj(qkvg, tables, ret_gn_gain[l].reshape(1, ret_width),
                            ret_gn_bias[l].reshape(1, ret_width), mix_sb, x2d,
                            w_out[l].astype(BF16), fgain,
                            (l == depth - 1), seq, ret_heads)
    return x2d.reshape(batch, seq, d_model)
```

```python
import functools

import jax
import jax.numpy as jnp
import numpy as np
from jax import lax
from jax.experimental import pallas as pl
from jax.experimental.pallas import tpu as pltpu

HEAD_DIM = 128
CHUNK = 128
ROPE_THETA = 10000.0
EPS = 1e-6
LOG2E = 1.4426950408889634
MASKED_LOGIT = -1e30
UNDERFLOW_LOG = 105.0
N_GROUPS = 8
IN_PROJ_ROW_SLAB = 256
RETENTION_HEADS_AT_ONCE = 4
OUT_PROJ_SLAB_CHUNKS = 2

F32 = jnp.float32
BF16 = jnp.bfloat16

VMEM_LIMIT_BYTES = 60 * 1024 * 1024


def _nt_dot(a, b):
    return lax.dot_general(a, b, (((1,), (1,)), ((), ())), preferred_element_type=F32)


def _tn_dot(a, b):
    return lax.dot_general(a, b, (((0,), (0,)), ((), ())), preferred_element_type=F32)


def _silu(g):
    return (0.5 * g) * (1.0 + jnp.tanh(0.5 * g))


EPILOGUE_ROTARY, EPILOGUE_ROTARY_SCALED, EPILOGUE_SCALED, EPILOGUE_IDENTITY = range(4)
GROUP_EPILOGUE = (EPILOGUE_ROTARY, EPILOGUE_ROTARY_SCALED, EPILOGUE_IDENTITY, EPILOGUE_IDENTITY,
                  EPILOGUE_SCALED, EPILOGUE_IDENTITY, EPILOGUE_IDENTITY, EPILOGUE_IDENTITY)


def _in_proj_kernel(kind_tab, x_ref, gain_ref, w_ref, *refs):
    n_tables = (len(refs) - 2) // 2
    cos_refs, sin_refs = refs[:n_tables], refs[n_tables:2 * n_tables]
    out_ref, h_ref = refs[2 * n_tables:]
    heads_per_group = out_ref.shape[0] // n_tables

    @pl.when(pl.program_id(1) == 0)
    def _():
        xf = x_ref[...]
        ms = jnp.mean(xf * xf, axis=-1, keepdims=True)
        h_ref[...] = ((xf * lax.rsqrt(ms + EPS)) * gain_ref[...]).astype(BF16)

    def project(with_sin):
        w = w_ref[...]
        for r in range(0, h_ref.shape[0], IN_PROJ_ROW_SLAB):
            rows = slice(r, r + IN_PROJ_ROW_SLAB)
            acc = jnp.dot(h_ref[rows, :], w, preferred_element_type=F32)
            for c in range(out_ref.shape[0]):
                blk = acc[:, c * HEAD_DIM:(c + 1) * HEAD_DIM]
                y = blk * cos_refs[c // heads_per_group][0, rows, :]
                if with_sin:
                    y = y + pltpu.roll(blk, HEAD_DIM // 2, 1) * sin_refs[c // heads_per_group][0, rows, :]
                out_ref[c, rows, :] = y.astype(BF16)

    j = pl.program_id(1)
    has_sin = kind_tab[j * n_tables] <= EPILOGUE_ROTARY_SCALED
    for g in range(1, n_tables):
        has_sin = has_sin | (kind_tab[j * n_tables + g] <= EPILOGUE_ROTARY_SCALED)
    pl.when(has_sin)(functools.partial(project, True))
    pl.when(jnp.logical_not(has_sin))(functools.partial(project, False))


def _epilogue_tables(seq):
    half = HEAD_DIM // 2
    inv = ROPE_THETA ** (-np.arange(half, dtype=np.float64) / half)
    ang = np.arange(seq, dtype=np.float64)[:, None] * inv[None, :]
    cos_h = np.cos(ang).astype(np.float32)
    sin_h = np.sin(ang).astype(np.float32)
    cos = np.concatenate([cos_h, cos_h], axis=-1)
    sin = np.concatenate([-sin_h, sin_h], axis=-1)
    ones = np.ones_like(cos)
    zeros = np.zeros_like(sin)
    scale = np.float32(HEAD_DIM ** -0.5)
    return (jnp.asarray(np.stack([cos, cos * scale, ones * scale, ones])),
            jnp.asarray(np.stack([sin, sin * scale, zeros, zeros])))


def _in_proj(x2d, gain, w_bf, cos_t, sin_t, seq, *, tm=1024, groups_per_tile=2):
    n_rows, d_model = x2d.shape
    in_cols = w_bf.shape[1]
    tn = groups_per_tile * (in_cols // N_GROUPS)
    heads_per_tile = tn // HEAD_DIM
    seq_tiles = seq // tm
    assert tm % IN_PROJ_ROW_SLAB == 0 and N_GROUPS % groups_per_tile == 0
    def table(g):
        return pl.BlockSpec((1, tm, HEAD_DIM),
                            lambda i, j, kind: (kind[j * groups_per_tile + g], i % seq_tiles, 0))
    tables = [table(g) for g in range(groups_per_tile)]
    grid_spec = pltpu.PrefetchScalarGridSpec(
        num_scalar_prefetch=1,
        grid=(n_rows // tm, N_GROUPS // groups_per_tile),
        in_specs=[
            pl.BlockSpec((tm, d_model), lambda i, j, kind: (i, 0)),
            pl.BlockSpec((1, d_model), lambda i, j, kind: (0, 0)),
            pl.BlockSpec((d_model, tn), lambda i, j, kind: (0, j)),
        ] + tables + tables,
        out_specs=pl.BlockSpec((heads_per_tile, tm, HEAD_DIM), lambda i, j, kind: (j, i, 0)),
        scratch_shapes=[pltpu.VMEM((tm, d_model), BF16)],
    )
    return pl.pallas_call(
        _in_proj_kernel,
        grid_spec=grid_spec,
        out_shape=jax.ShapeDtypeStruct((in_cols // HEAD_DIM, n_rows, HEAD_DIM), BF16),
        compiler_params=pltpu.CompilerParams(
            dimension_semantics=("arbitrary", "arbitrary"),
            vmem_limit_bytes=VMEM_LIMIT_BYTES),
        name="in_proj",
    )(jnp.asarray(GROUP_EPILOGUE, jnp.int32), x2d, gain, w_bf,
      *([cos_t] * groups_per_tile), *([sin_t] * groups_per_tile))


def _stickbreak_kernel(q_ref, k_ref, v_ref, g_ref, gain_ref, out_ref,
                       acc_ref, carry_ref, tri_ref, bias_ref, z_a, sp_a, z_b, sp_b, *, blk, width):
    seq = q_ref.shape[1]
    n_blk = seq // blk
    n_groups = n_blk // width
    row = lax.broadcasted_iota(jnp.int32, (blk, blk), 0)
    col = lax.broadcasted_iota(jnp.int32, (blk, blk), 1)
    tri_ref[...] = (row >= col).astype(BF16)
    bias_ref[...] = jnp.where(col < row, 0.0, MASKED_LOGIT)

    def rows(start, size):
        return pl.ds(start if isinstance(start, int) else pl.multiple_of(start, blk), size)

    def scores(tiles, z_buf, sp_buf):
        for slot, r0, q0, m, k0, n, diagonal, _ in tiles:
            z = _nt_dot(q_ref[0, rows(q0, m), :], k_ref[0, rows(k0, n), :])
            if diagonal:
                z = z + bias_ref[r0:r0 + m, 0:n]
            sp = jnp.maximum(z, 0.0) + jnp.log(1.0 + jnp.exp2(jnp.abs(z) * (-LOG2E)))
            z_buf[slot, r0:r0 + m, 0:n] = z
            sp_buf[slot, r0:r0 + m, 0:n] = sp.astype(BF16)

    def weights(tiles, z_buf, sp_buf, first):
        csums = [jnp.dot(sp_buf[slot, r0:r0 + m, 0:n], tri_ref[0:n, 0:n], preferred_element_type=F32)
                 for slot, r0, _, m, _, n, _, _ in tiles]
        if not first:
            carries = [carry_ref[rows(q0, m), :] for _, _, q0, m, _, _, _, _ in tiles]
            csums_c = [c + jnp.concatenate([carry] * (c.shape[1] // HEAD_DIM), axis=1)
                       for c, carry in zip(csums, carries)]
        else:
            csums_c = csums
        pvs = [jnp.dot(jnp.exp(z_buf[slot, r0:r0 + m, 0:n] - c).astype(BF16), v_ref[0, rows(k0, n), :],
                       preferred_element_type=F32)
               for (slot, r0, _, m, k0, n, _, _), c in zip(tiles, csums_c)]
        if not first:
            accs = [acc_ref[rows(q0, m), :] for _, _, q0, m, _, _, _, _ in tiles]
        for t, (_, _, q0, m, _, _, _, weight) in enumerate(tiles):
            total = jnp.broadcast_to(csums[t][:, 0:1], (m, HEAD_DIM))
            pv = pvs[t]
            if weight is not None:
                total = weight * total
                pv = weight * pv
            carry_ref[rows(q0, m), :] = total if first else carries[t] + total
            acc_ref[rows(q0, m), :] = pv if first else accs[t] + pv

    buffers = ((z_a, sp_a), (z_b, sp_b))
    half = blk // 2

    def diagonal_tiles(slot, i):
        return [(slot, 0, i * blk, half, i * blk, half, True, None),
                (slot, half, i * blk + half, half, i * blk, blk, True, None)]

    near = ([[tile for t in range(width) for tile in diagonal_tiles(t, n * width + t)]
             for n in range(n_groups)]
            + [[(i % width, 0, i * blk, blk, (i - 1) * blk, blk, False, None)
                for i in range(n * width, (n + 1) * width) if i >= 1] for n in range(n_groups)])
    scores(near[0], *buffers[0])
    for g, tiles in enumerate(near):
        if g + 1 < len(near):
            scores(near[g + 1], *buffers[(g + 1) % 2])
        weights(tiles, *buffers[g % 2], first=(g < n_groups))

    def group_live(n, d):
        live = jnp.bool_(False)
        for i in range(n * width, (n + 1) * width):
            live = live | ((i >= d) & (jnp.min(carry_ref[rows(i * blk, blk), :]) < UNDERFLOW_LOG))
        return live

    def any_live(d):
        live = jnp.bool_(False)
        for n in range(n_groups):
            live = live | group_live(n, d)
        return live

    def far_wave(d):
        for n in range(n_groups):
            @pl.when(group_live(n, d))
            def _():
                tiles = [(i % width, 0, i * blk, blk, jnp.maximum(i - d, 0) * blk, blk, False,
                          (i >= d).astype(F32)) for i in range(n * width, (n + 1) * width)]
                scores(tiles, *buffers[0])
                weights(tiles, *buffers[0], first=False)
        return d + 1

    lax.while_loop(lambda d: (d < n_blk) & any_live(d), far_wave, jnp.int32(2))

    def finish(i, _):
        block = rows(i * blk, blk)
        acc = acc_ref[block, :]
        ms = jnp.mean(acc * acc, axis=-1, keepdims=True)
        y = (acc * lax.rsqrt(ms + EPS)) * gain_ref[...]
        gate = g_ref[0, block, :].astype(F32)
        out_ref[block, :] = (_silu(gate) * y).astype(BF16)
        return 0

    lax.fori_loop(0, n_blk, finish, 0, unroll=True)


def _stickbreak(qkvg, gain, batch, seq, n_heads, first_group, *, blk=256, width=4):
    n_rows = batch * seq
    assert (seq // blk) % width == 0
    def slab(group):
        return pl.BlockSpec((1, seq, HEAD_DIM),
                            lambda b, h, group=group: ((first_group + group) * n_heads + h, b, 0))
    kernel = functools.partial(_stickbreak_kernel, blk=blk, width=width)
    stage_buffers = [pltpu.VMEM((width, blk, blk), F32),
                     pltpu.VMEM((width, blk, blk), BF16)]
    return pl.pallas_call(
        kernel,
        grid=(batch, n_heads),
        in_specs=[slab(0), slab(1), slab(2), slab(3),
                  pl.BlockSpec((1, HEAD_DIM), lambda b, h: (0, h))],
        out_specs=pl.BlockSpec((seq, HEAD_DIM), lambda b, h: (b, h)),
        out_shape=jax.ShapeDtypeStruct((n_rows, n_heads * HEAD_DIM), BF16),
        scratch_shapes=[pltpu.VMEM((seq, HEAD_DIM), F32),
                        pltpu.VMEM((seq, HEAD_DIM), F32),
                        pltpu.VMEM((blk, blk), BF16),
                        pltpu.VMEM((blk, blk), F32),
                        ] + stage_buffers + stage_buffers,
        compiler_params=pltpu.CompilerParams(
            dimension_semantics=("arbitrary", "arbitrary"),
            vmem_limit_bytes=VMEM_LIMIT_BYTES),
        name="stickbreak",
    )(qkvg, qkvg, qkvg, qkvg, gain)


def _ret_out_proj_kernel(q_ref, k_ref, v_ref, g_ref, decay_ref, xi_ref, zeta_ref, gc_ref,
                         gn_g_ref, gn_b_ref, sb_ref, x_ref, w_ret_ref, w_sb_ref, fgain_ref, out_ref,
                         r_ref, mix_a, mix_b, *, final_norm, tiles_per_seq):
    s = pl.program_id(0)
    n_tiles = pl.num_programs(0) - 1
    n_heads = q_ref.shape[0]
    n_chunks = q_ref.shape[1] // CHUNK

    @pl.when(s == 0)
    def _():
        r_ref[...] = jnp.zeros_like(r_ref)
        mix_b[...] = jnp.zeros_like(mix_b)

    tile = jnp.minimum(s, n_tiles - 1)
    keep = (tile % tiles_per_seq != 0).astype(F32)

    def retention_chunk(c, mix_out):
        for h0 in range(0, n_heads, RETENTION_HEADS_AT_ONCE):
            retention_heads(c, range(h0, h0 + RETENTION_HEADS_AT_ONCE), mix_out)

    def retention_heads(c, heads, mix_out):
        rows = slice(c * CHUNK, (c + 1) * CHUNK)
        qc = {h: q_ref[h, rows, :] for h in heads}
        kc = {h: k_ref[h, rows, :] for h in heads}
        vc = {h: v_ref[h, rows, :] for h in heads}
        rs = {h: r_ref[h] * keep if c == 0 else r_ref[h] for h in heads}
        scores = {h: _nt_dot(qc[h], kc[h]) for h in heads}
        kvs = {h: _tn_dot((kc[h].astype(F32) * zeta_ref[h]).astype(BF16), vc[h]) for h in heads}
        cross = {h: jnp.dot((qc[h].astype(F32) * xi_ref[h]).astype(BF16), rs[h].astype(BF16),
                            preferred_element_type=F32) for h in heads}
        intra = {h: jnp.dot((scores[h] * decay_ref[h]).astype(BF16), vc[h], preferred_element_type=F32)
                 for h in heads}
        for h in heads:
            cols = slice(h * HEAD_DIM, (h + 1) * HEAD_DIM)
            r_ref[h] = gc_ref[h] * rs[h] + kvs[h]
            o = intra[h] + cross[h]
            mu = jnp.mean(o, axis=-1, keepdims=True)
            d = o - mu
            var = jnp.mean(d * d, axis=-1, keepdims=True)
            y = (d * lax.rsqrt(var + EPS)) * gn_g_ref[:, cols] + gn_b_ref[:, cols]
            gate = g_ref[h, rows, :].astype(F32)
            mix_out[rows, cols] = (_silu(gate) * y).astype(BF16)

    def project_rows(c, mix_in):
        rows = slice(c * CHUNK, (c + OUT_PROJ_SLAB_CHUNKS) * CHUNK)
        y = (jnp.dot(mix_in[rows, :], w_ret_ref[...], preferred_element_type=F32)
             + jnp.dot(sb_ref[rows, :], w_sb_ref[...], preferred_element_type=F32))
        xn = x_ref[rows, :] + y
        if final_norm:
            ms = jnp.mean(xn * xn, axis=-1, keepdims=True)
            xn = (xn * lax.rsqrt(ms + EPS)) * fgain_ref[...]
        out_ref[rows, :] = xn

    def step(mix_out, mix_in):
        for c0 in range(0, n_chunks, OUT_PROJ_SLAB_CHUNKS):
            for c in range(c0, c0 + OUT_PROJ_SLAB_CHUNKS):
                retention_chunk(c, mix_out)
            project_rows(c0, mix_in)

    @pl.when(s % 2 == 0)
    def _():
        step(mix_a, mix_b)

    @pl.when(s % 2 == 1)
    def _():
        step(mix_b, mix_a)


def _ret_out_proj(qkvg, tables, gn_gain, gn_bias, mix_sb, x2d, w_out, fgain, final_norm,
                  seq, n_heads, *, tm=512):
    decay, xi_b, zeta_b, gc_b = tables
    n_rows, d_model = x2d.shape
    ret_w = n_heads * HEAD_DIM
    sb_w = mix_sb.shape[1]
    n_tiles = n_rows // tm
    assert (tm // CHUNK) % OUT_PROJ_SLAB_CHUNKS == 0 and n_heads % RETENTION_HEADS_AT_ONCE == 0
    assert ret_w == sb_w and w_out.shape[0] == ret_w + sb_w
    def slab(group):
        return pl.BlockSpec((n_heads, tm, HEAD_DIM),
                            lambda s, group=group: (group, jnp.minimum(s, n_tiles - 1), 0))
    whole = lambda s: (0, 0, 0)
    prev = lambda s: (jnp.maximum(s - 1, 0), 0)
    kernel = functools.partial(_ret_out_proj_kernel, final_norm=final_norm, tiles_per_seq=seq // tm)
    return pl.pallas_call(
        kernel,
        grid=(n_tiles + 1,),
        in_specs=[
            slab(0), slab(1), slab(2), slab(3),
            pl.BlockSpec((n_heads, CHUNK, CHUNK), whole),
            pl.BlockSpec((n_heads, CHUNK, HEAD_DIM), whole),
            pl.BlockSpec((n_heads, CHUNK, HEAD_DIM), whole),
            pl.BlockSpec((n_heads, 1, HEAD_DIM), whole),
            pl.BlockSpec((1, ret_w), lambda s: (0, 0)),
            pl.BlockSpec((1, ret_w), lambda s: (0, 0)),
            pl.BlockSpec((tm, sb_w), prev),
            pl.BlockSpec((tm, d_model), prev),
            pl.BlockSpec((ret_w, d_model), lambda s: (0, 0)),
            pl.BlockSpec((sb_w, d_model), lambda s: (1, 0)),
            pl.BlockSpec((1, d_model), lambda s: (0, 0)),
        ],
        out_specs=pl.BlockSpec((tm, d_model), prev),
        out_shape=jax.ShapeDtypeStruct((n_rows, d_model), F32),
        scratch_shapes=[pltpu.VMEM((n_heads, HEAD_DIM, HEAD_DIM), F32),
                        pltpu.VMEM((tm, ret_w), BF16),
                        pltpu.VMEM((tm, ret_w), BF16)],
        compiler_params=pltpu.CompilerParams(
            dimension_semantics=("arbitrary",),
            vmem_limit_bytes=VMEM_LIMIT_BYTES),
        name="ret_out_proj",
    )(qkvg, qkvg, qkvg, qkvg, decay, xi_b, zeta_b, gc_b, gn_gain, gn_bias,
      mix_sb, x2d, w_out, w_out, fgain)


def _retention_tables(n_heads):
    lg = jnp.log1p(-jnp.exp2(-5.0 - jnp.arange(n_heads, dtype=F32)))
    n = jnp.arange(CHUNK, dtype=F32)
    rel = n[:, None] - n[None, :]
    decay = jnp.where(rel >= 0, jnp.exp(lg[:, None, None] * jnp.maximum(rel, 0.0)), 0.0)
    xi = jnp.exp(lg[:, None] * (n + 1.0))
    zeta = jnp.exp(lg[:, None] * (CHUNK - 1.0 - n))
    gamma_c = jnp.exp(lg * CHUNK)
    xi_b = jnp.broadcast_to(xi[:, :, None], (n_heads, CHUNK, HEAD_DIM))
    zeta_b = jnp.broadcast_to(zeta[:, :, None], (n_heads, CHUNK, HEAD_DIM))
    gc_b = jnp.broadcast_to(gamma_c[:, None, None], (n_heads, 1, HEAD_DIM))
    return decay, xi_b, zeta_b, gc_b


def kernel(x, norm_gain, w_in, ret_gn_gain, ret_gn_bias, sb_norm_gain, w_out, final_norm_gain):
    batch, seq, d_model = x.shape
    depth = norm_gain.shape[0]
    ret_width = ret_gn_gain.shape[1]
    sb_width = sb_norm_gain.shape[1]
    ret_heads = ret_width // HEAD_DIM
    sb_heads = sb_width // HEAD_DIM
    assert ret_heads == sb_heads and w_in.shape[2] == N_GROUPS * ret_width

    cos_t, sin_t = _epilogue_tables(seq)
    tables = _retention_tables(ret_heads)
    fgain = final_norm_gain.reshape(1, d_model)
    x2d = x.reshape(batch * seq, d_model)
    for l in range(depth):
        qkvg = _in_proj(x2d, norm_gain[l].reshape(1, d_model), w_in[l].astype(BF16), cos_t, sin_t, seq)
        mix_sb = _stickbreak(qkvg, sb_norm_gain[l].reshape(1, sb_width), batch, seq, sb_heads, 4)
        x2d = _ret_out_proj(qkvg, tables, ret_gn_gain[l].reshape(1, ret_width),
                            ret_gn_bias[l].reshape(1, ret_width), mix_sb, x2d,
                            w_out[l].astype(BF16), fgain,
                            (l == depth - 1), seq, ret_heads)
    return x2d.reshape(batch, seq, d_model)
```
